```python
import math
import jax, jax.numpy as jnp
from jax import lax
import numpy as np

D_MODEL = 1024
BATCH = 8
SEQ = 8192
DEPTH = 2

MIX_W = 512
HG_HEADS = 4
HG_DH = MIX_W // HG_HEADS
S5_GROUP = 16
S5_GROUPS = MIX_W // S5_GROUP
S5_STATE = 64
DN_HEADS = 4
DN_DH = MIX_W // DN_HEADS
CONV_K = 4
GLA_HEADS = 4
GLA_DK = 64
GLA_DV = MIX_W // GLA_HEADS
GLA_RANK = 16
GLA_NORMALIZER = 16.0
CHUNK = 64
N_BRANCH = 4
FFN_DIM = 2816
N_EXPERTS = 8
TOP_K = 2
EXPERT_DIM = 3584
MOE_BLOCK = 256
N_DENSE = (DEPTH + 1) // 2
N_MOE = DEPTH // 2
NORM_EPS = 1e-6
IN_SPLITS = (MIX_W, MIX_W, MIX_W, MIX_W,
             MIX_W,
             3 * MIX_W, DN_HEADS, DN_HEADS, MIX_W,
             GLA_HEADS * GLA_DK, GLA_HEADS * GLA_DK, MIX_W, GLA_RANK, MIX_W,
             N_BRANCH * D_MODEL)
IN_COLS = sum(IN_SPLITS)

kernel_name = 'hybrid_hgrn2_s5_gdn_gla_moe'


def rmsnorm(x, w):
    xf = x.astype(jnp.float32)
    y = xf * lax.rsqrt(jnp.mean(xf * xf, axis=-1, keepdims=True) + NORM_EPS)
    return (y * w.astype(jnp.float32)).astype(x.dtype)


def l2norm(x):
    return x * lax.rsqrt(jnp.sum(x * x, axis=-1, keepdims=True) + NORM_EPS)


def split_cols(p, sizes):
    out, start = [], 0
    for s in sizes:
        out.append(p[..., start:start + s])
        start += s
    return out


def to_chunks(t):
    bsz, seq = t.shape[:2]
    rest = t.shape[2:]
    t = t.astype(jnp.float32).reshape((bsz, seq // CHUNK, CHUNK) + rest)
    return t.transpose((1, 0, 3, 2) + tuple(range(4, t.ndim)))


def from_chunks(o):
    n, bsz, nh, c, d = o.shape
    return o.transpose(1, 0, 3, 2, 4).reshape(bsz, n * c, nh, d)


def chunk_gla(q, k, v, log_decay):
    qc, kc, vc = to_chunks(q), to_chunks(k), to_chunks(v)
    gc = jnp.cumsum(to_chunks(log_decay), axis=3)
    causal = jnp.tril(jnp.ones((CHUNK, CHUNK), dtype=bool))[:, :, None]

    def step(state, inp):
        q_i, k_i, v_i, g_i = inp
        decay = jnp.exp(jnp.where(causal, g_i[:, :, :, None, :] - g_i[:, :, None, :, :], -jnp.inf))
        scores = jnp.sum(q_i[:, :, :, None, :] * k_i[:, :, None, :, :] * decay, axis=-1)
        o = (jnp.einsum('bhid,bhde->bhie', q_i * jnp.exp(g_i), state)
             + jnp.einsum('bhij,bhje->bhie', scores, v_i))
        g_last = g_i[:, :, -1:, :]
        state = (state * jnp.exp(g_last[:, :, 0, :, None])
                 + jnp.einsum('bhjd,bhje->bhde', k_i * jnp.exp(g_last - g_i), v_i))
        return state, o

    n, bsz, nh, _, dk = qc.shape
    state0 = jnp.zeros((bsz, nh, dk, vc.shape[-1]), jnp.float32)
    _, o = lax.scan(step, state0, (qc, kc, vc, gc))
    return from_chunks(o)


def chunk_gated_delta(q, k, v, log_decay, beta):
    qc, kc, vc = to_chunks(q), to_chunks(k), to_chunks(v)
    bc = to_chunks(beta)[..., None]
    g = jnp.cumsum(to_chunks(log_decay), axis=-1)
    incl = jnp.tril(jnp.ones((CHUNK, CHUNK), dtype=bool))
    strict = jnp.tril(jnp.ones((CHUNK, CHUNK), dtype=bool), k=-1)
    decay = jnp.exp(jnp.where(incl, g[..., :, None] - g[..., None, :], -jnp.inf))
    kb = kc * bc
    lower = jnp.where(strict, jnp.einsum('nbhid,nbhjd->nbhij', kb, kc) * decay, 0.0)
    eye = jnp.eye(CHUNK, dtype=jnp.float32)
    t_inv = lax.linalg.triangular_solve(eye + lower, jnp.broadcast_to(eye, lower.shape),
                                        left_side=True, lower=True)
    u = jnp.einsum('nbhij,nbhje->nbhie', t_inv, vc * bc)
    w = jnp.einsum('nbhij,nbhjd->nbhid', t_inv, kb * jnp.exp(g)[..., None])
    qk = jnp.einsum('nbhid,nbhjd->nbhij', qc, kc) * decay
    q_dec = qc * jnp.exp(g)[..., None]
    k_dec = kc * jnp.exp(g[..., -1:] - g)[..., None]
    chunk_decay = jnp.exp(g[..., -1])

    def step(state, inp):
        q_i, w_i, u_i, qk_i, k_i, a_i = inp
        v_new = u_i - jnp.einsum('bhid,bhde->bhie', w_i, state)
        o = jnp.einsum('bhid,bhde->bhie', q_i, state) + jnp.einsum('bhij,bhje->bhie', qk_i, v_new)
        state = state * a_i[..., None, None] + jnp.einsum('bhjd,bhje->bhde', k_i, v_new)
        return state, o

    n, bsz, nh, _, dk = qc.shape
    state0 = jnp.zeros((bsz, nh, dk, vc.shape[-1]), jnp.float32)
    _, o = lax.scan(step, state0, (q_dec, w, u, qk, k_dec, chunk_decay))
    return from_chunks(o)


def _complex_affine_combine(c1, c2):
    a1r, a1i, b1r, b1i = c1
    a2r, a2i, b2r, b2i = c2
    return (a2r * a1r - a2i * a1i, a2r * a1i + a2i * a1r,
            a2r * b1r - a2i * b1i + b2r, a2r * b1i + a2i * b1r + b2i)


def s5_ssm(u, a_re, a_im, log_dt, b_re, b_im, c_re, c_im, d_skip):
    f32 = jnp.float32
    bsz, seq, _ = u.shape
    uf = u.astype(f32).reshape(bsz, seq, S5_GROUPS, S5_GROUP)
    ar, ai = a_re.astype(f32), a_im.astype(f32)
    dt = jnp.exp(log_dt.astype(f32))[:, None]
    mag = jnp.exp(ar * dt)
    abar_r, abar_i = mag * jnp.cos(ai * dt), mag * jnp.sin(ai * dt)
    den = ar * ar + ai * ai
    nr, ni = abar_r - 1.0, abar_i
    zr, zi = (nr * ar + ni * ai) / den, (ni * ar - nr * ai) / den
    br, bi = b_re.astype(f32), b_im.astype(f32)
    bbar_r = zr[..., None] * br - zi[..., None] * bi
    bbar_i = zr[..., None] * bi + zi[..., None] * br
    bu_r = jnp.einsum('bsgh,gph->bsgp', uf, bbar_r)
    bu_i = jnp.einsum('bsgh,gph->bsgp', uf, bbar_i)
    shape_a = (1, seq, S5_GROUPS, S5_STATE)
    _, _, xr, xi = lax.associative_scan(
        _complex_affine_combine,
        (jnp.broadcast_to(abar_r, shape_a), jnp.broadcast_to(abar_i, shape_a), bu_r, bu_i), axis=1)
    y = (jnp.einsum('bsgp,ghp->bsgh', xr, c_re.astype(f32))
         - jnp.einsum('bsgp,ghp->bsgh', xi, c_im.astype(f32))
         + d_skip.astype(f32).reshape(S5_GROUPS, S5_GROUP) * uf)
    return y.reshape(bsz, seq, MIX_W)


def causal_conv(x, w):
    k_w, ch = w.shape
    return lax.conv_general_dilated(x, w.astype(x.dtype)[:, None, :], window_strides=(1,),
                                    padding=((k_w - 1, 0),), dimension_numbers=('NWC', 'WIO', 'NWC'),
                                    feature_group_count=ch)


def gated_head_norm(o, w, gate):
    o = o * lax.rsqrt(jnp.mean(o * o, axis=-1, keepdims=True) + NORM_EPS) * w.astype(jnp.float32)
    bsz, seq = o.shape[:2]
    return (o.reshape(bsz, seq, -1) * jax.nn.silu(gate.astype(jnp.float32))).astype(gate.dtype)


def swiglu(h, w_gate, w_up, w_down):
    return (jax.nn.silu(h @ w_gate) * (h @ w_up)) @ w_down


def moe_swiglu(h, router, w_gate, w_up, w_down):
    bsz, seq, d = h.shape
    n_tok = bsz * seq
    n_assign = n_tok * TOP_K
    hf = h.reshape(n_tok, d)
    logits = (hf @ router).astype(jnp.float32)
    top_logit, top_idx = lax.top_k(logits, TOP_K)
    gate = jax.nn.softmax(top_logit, axis=-1)
    e_flat = top_idx.reshape(-1)
    tok_flat = jnp.arange(n_assign, dtype=jnp.int32) // TOP_K
    order = jnp.argsort(e_flat)
    e_s, tok_s, g_s = e_flat[order], tok_flat[order], gate.reshape(-1)[order]
    counts = jnp.bincount(e_flat, length=N_EXPERTS)
    padded = (counts + MOE_BLOCK - 1) // MOE_BLOCK * MOE_BLOCK
    pad_end = jnp.cumsum(padded)
    pad_start = pad_end - padded
    grp_start = jnp.cumsum(counts) - counts
    dest = pad_start[e_s] + jnp.arange(n_assign, dtype=jnp.int32) - grp_start[e_s]
    n_blocks = -(-n_assign // MOE_BLOCK) + N_EXPERTS
    buf = jnp.zeros((n_blocks * MOE_BLOCK, d), h.dtype).at[dest].set(hf[tok_s])
    block_e = jnp.minimum(jnp.searchsorted(pad_end, jnp.arange(n_blocks) * MOE_BLOCK, side='right'),
                          N_EXPERTS - 1)

    def expert_block(args):
        xb, e = args
        return (jax.nn.silu(xb @ w_gate[e]) * (xb @ w_up[e])) @ w_down[e]

    yb = lax.map(expert_block, (buf.reshape(n_blocks, MOE_BLOCK, d), block_e))
    y_s = yb.reshape(-1, d)[dest] * g_s[:, None].astype(h.dtype)
    out = jnp.zeros((n_tok, d), h.dtype).at[tok_s].add(y_s)
    return out.reshape(bsz, seq, d)


def token_mix(xn, w_in, lb, hgrn_norm, s5_a_re, s5_a_im, s5_log_dt, s5_b_re, s5_b_im, s5_c_re, s5_c_im,
              s5_d, s5_glu_w, s5_glu_b, dn_conv_w, dn_a_log, dn_dt_bias, dn_norm, gla_w_lr, gla_b_lr,
              gla_norm, w_branch, w_out):
    f32 = jnp.float32
    bsz, seq, _ = xn.shape
    dt = xn.dtype

    def heads(t, n):
        return t.reshape(bsz, seq, n, -1)

    (hq, hf, hi, hgate, su, dqkv, dbeta, ddecay, dgate,
     gq, gk, gv, glr, ggate, mgate) = split_cols(xn @ w_in, IN_SPLITS)

    zf = hf.astype(f32)
    forget = lb + (1.0 - lb) * jax.nn.sigmoid(zf)
    log_f = jnp.log(forget)
    k_a = (1.0 - lb) * jax.nn.sigmoid(-zf)
    o_a = chunk_gla(heads(hq.astype(f32), HG_HEADS) * HG_DH ** -0.5, heads(k_a, HG_HEADS),
                    heads(hi, HG_HEADS), heads(log_f, HG_HEADS))
    y_a = gated_head_norm(o_a, hgrn_norm, hgate)

    z = jax.nn.gelu(s5_ssm(su, s5_a_re, s5_a_im, s5_log_dt, s5_b_re, s5_b_im, s5_c_re, s5_c_im, s5_d))
    y_b = (z * jax.nn.sigmoid(z @ s5_glu_w.astype(f32) + s5_glu_b.astype(f32))).astype(dt)

    c = jax.nn.silu(causal_conv(dqkv, dn_conv_w))
    cq, ck, cv = split_cols(c, (MIX_W, MIX_W, MIX_W))
    q_c = l2norm(heads(cq.astype(f32), DN_HEADS)) * DN_DH ** -0.5
    k_c = l2norm(heads(ck.astype(f32), DN_HEADS))
    beta = jax.nn.sigmoid(dbeta.astype(f32))
    g_c = -jnp.exp(dn_a_log.astype(f32)) * jax.nn.softplus(ddecay.astype(f32) + dn_dt_bias.astype(f32))
    o_c = chunk_gated_delta(q_c, k_c, heads(cv, DN_HEADS), g_c, beta)
    y_c = gated_head_norm(o_c, dn_norm, dgate)

    log_a = jax.nn.log_sigmoid((glr @ gla_w_lr + gla_b_lr).astype(f32)) / GLA_NORMALIZER
    o_d = chunk_gla(heads(gq.astype(f32), GLA_HEADS) * GLA_DK ** -0.5, heads(gk, GLA_HEADS),
                    heads(gv, GLA_HEADS), heads(log_a, GLA_HEADS))
    y_d = gated_head_norm(o_d, gla_norm, ggate)

    mixed = jnp.zeros((bsz, seq, D_MODEL), f32)
    for n, y in enumerate((y_a, y_b, y_c, y_d)):
        gate = jax.nn.sigmoid(mgate[..., n * D_MODEL:(n + 1) * D_MODEL].astype(f32))
        mixed = mixed + gate * (y @ w_branch[n]).astype(f32)
    return mixed.astype(dt) @ w_out


def setup_inputs(seed: int = 0) -> dict:
    key = jax.random.key(seed)
    keys = iter(jax.random.split(key, 48))
    f32 = jnp.float32

    def nrm(shape, scale):
        return scale * jax.random.normal(next(keys), shape, f32)

    def unif(shape, lo, hi):
        return jax.random.uniform(next(keys), shape, f32, lo, hi)

    def gain(shape):
        return 1.0 + nrm(shape, 0.01)

    dt_dn = jnp.exp(unif((DEPTH, DN_HEADS), math.log(1e-3), math.log(1e-1)))
    return {
        'x': nrm((BATCH, SEQ, D_MODEL), 1.0),
        'attn_norm': gain((DEPTH, D_MODEL)),
        'ffn_norm': gain((DEPTH, D_MODEL)),
        'final_norm': gain((D_MODEL,)),
        'w_in': nrm((DEPTH, D_MODEL, IN_COLS), D_MODEL ** -0.5),
        'hgrn_lb': nrm((DEPTH, MIX_W), 0.1),
        'hgrn_norm': gain((DEPTH, HG_DH)),
        's5_a_re': -0.5 + nrm((DEPTH, S5_GROUPS, S5_STATE), 0.01),
        's5_a_im': math.pi * jnp.arange(S5_STATE, dtype=f32) + nrm((DEPTH, S5_GROUPS, S5_STATE), 0.01),
        's5_log_dt': unif((DEPTH, S5_GROUPS), math.log(1e-3), math.log(1e-1)),
        's5_b_re': nrm((DEPTH, S5_GROUPS, S5_STATE, S5_GROUP), (2 * S5_GROUP) ** -0.5),
        's5_b_im': nrm((DEPTH, S5_GROUPS, S5_STATE, S5_GROUP), (2 * S5_GROUP) ** -0.5),
        's5_c_re': nrm((DEPTH, S5_GROUPS, S5_GROUP, S5_STATE), (2 * S5_STATE) ** -0.5),
        's5_c_im': nrm((DEPTH, S5_GROUPS, S5_GROUP, S5_STATE), (2 * S5_STATE) ** -0.5),
        's5_d': nrm((DEPTH, MIX_W), 1.0),
        's5_glu_w': nrm((DEPTH, MIX_W, MIX_W), MIX_W ** -0.5),
        's5_glu_b': nrm((DEPTH, MIX_W), 0.01),
        'dn_conv_w': nrm((DEPTH, CONV_K, 3 * MIX_W), CONV_K ** -0.5),
        'dn_a_log': jnp.log(unif((DEPTH, DN_HEADS), 1.0, 16.0)),
        'dn_dt_bias': dt_dn + jnp.log(-jnp.expm1(-dt_dn)),
        'dn_norm': gain((DEPTH, DN_DH)),
        'gla_w_lr': nrm((DEPTH, GLA_RANK, GLA_HEADS * GLA_DK), GLA_RANK ** -0.5),
        'gla_b_lr': nrm((DEPTH, GLA_HEADS * GLA_DK), 0.01),
        'gla_norm': gain((DEPTH, GLA_DV)),
        'w_branch': nrm((DEPTH, N_BRANCH, MIX_W, D_MODEL), MIX_W ** -0.5),
        'w_out': nrm((DEPTH, D_MODEL, D_MODEL), D_MODEL ** -0.5),
        'ffn_w_gate': nrm((N_DENSE, D_MODEL, FFN_DIM), D_MODEL ** -0.5),
        'ffn_w_up': nrm((N_DENSE, D_MODEL, FFN_DIM), D_MODEL ** -0.5),
        'ffn_w_down': nrm((N_DENSE, FFN_DIM, D_MODEL), FFN_DIM ** -0.5),
        'moe_router': nrm((N_MOE, D_MODEL, N_EXPERTS), D_MODEL ** -0.5),
        'moe_w_gate': nrm((N_MOE, N_EXPERTS, D_MODEL, EXPERT_DIM), D_MODEL ** -0.5),
        'moe_w_up': nrm((N_MOE, N_EXPERTS, D_MODEL, EXPERT_DIM), D_MODEL ** -0.5),
        'moe_w_down': nrm((N_MOE, N_EXPERTS, EXPERT_DIM, D_MODEL), EXPERT_DIM ** -0.5),
    }


def reference(x, attn_norm, ffn_norm, final_norm, w_in, hgrn_lb, hgrn_norm, s5_a_re, s5_a_im, s5_log_dt,
              s5_b_re, s5_b_im, s5_c_re, s5_c_im, s5_d, s5_glu_w, s5_glu_b, dn_conv_w, dn_a_log, dn_dt_bias,
              dn_norm, gla_w_lr, gla_b_lr, gla_norm, w_branch, w_out, ffn_w_gate, ffn_w_up, ffn_w_down,
              moe_router, moe_w_gate, moe_w_up, moe_w_down):
    lb_soft = jax.nn.softmax(hgrn_lb.astype(jnp.float32), axis=0)
    lb_all = jnp.cumsum(lb_soft, axis=0) - lb_soft[0]
    h = x
    for layer in range(DEPTH):
        xn = rmsnorm(h, attn_norm[layer])
        h = h + token_mix(xn, w_in[layer], lb_all[layer], hgrn_norm[layer], s5_a_re[layer], s5_a_im[layer],
                          s5_log_dt[layer], s5_b_re[layer], s5_b_im[layer], s5_c_re[layer], s5_c_im[layer],
                          s5_d[layer], s5_glu_w[layer], s5_glu_b[layer], dn_conv_w[layer], dn_a_log[layer],
                          dn_dt_bias[layer], dn_norm[layer], gla_w_lr[layer], gla_b_lr[layer], gla_norm[layer],
                          w_branch[layer], w_out[layer])
        hn = rmsnorm(h, ffn_norm[layer])
        j = layer // 2
        if layer % 2 == 0:
            h = h + swiglu(hn, ffn_w_gate[j], ffn_w_up[j], ffn_w_down[j])
        else:
            h = h + moe_swiglu(hn, moe_router[j], moe_w_gate[j], moe_w_up[j], moe_w_down[j])
    return rmsnorm(h, final_norm)
```

```python
import functools
import math

import numpy as np
import jax
import jax.numpy as jnp
from jax import lax
from jax.experimental import pallas as pl
from jax.experimental.pallas import tpu as pltpu

F32 = jnp.float32
BF16 = jnp.bfloat16
HIGHEST = lax.Precision.HIGHEST

D_MODEL = 1024
MIX_W = 512
HG_HEADS = 4
HG_DH = MIX_W // HG_HEADS
S5_GROUP = 16
S5_GROUPS = MIX_W // S5_GROUP
S5_STATE = 64
DN_HEADS = 4
DN_DH = MIX_W // DN_HEADS
CONV_K = 4
GLA_HEADS = 4
GLA_DK = 64
GLA_DV = MIX_W // GLA_HEADS
GLA_RANK = 16
GLA_NORMALIZER = 16.0
CHUNK = 64
N_BRANCH = 4
N_EXPERTS = 8
TOP_K = 2
NORM_EPS = 1e-6

LANE = 128
SUBLANE = 8
VMEM_LIMIT = 56 * 1024 * 1024

MIX_TILE = 256
S5_STEPS = 64
S5_CLUSTER = 8
ROW_TILE = 512
MOE_TILE = 512
LEVELS = (32, 16, 8, 4, 2, 1)


def _cparams(n_axes):
    return pltpu.CompilerParams(dimension_semantics=("arbitrary",) * n_axes, vmem_limit_bytes=VMEM_LIMIT)


def _rms(x, w):
    return x * lax.rsqrt(jnp.mean(x * x, axis=-1, keepdims=True) + NORM_EPS) * w


def _dot(a, b):
    return jnp.dot(a, b, preferred_element_type=F32)


def _dot_nt(a, b, precision=None):
    return lax.dot_general(a, b, (((1,), (1,)), ((), ())), preferred_element_type=F32, precision=precision)


def _dot_tn(a, b):
    return lax.dot_general(a, b, (((0,), (0,)), ((), ())), preferred_element_type=F32)


def _split3(x):
    hi = x.astype(BF16)
    r1 = x - hi.astype(F32)
    mid = r1.astype(BF16)
    lo = (r1 - mid.astype(F32)).astype(BF16)
    return hi, mid, lo


def _range_sum_matrix():
    c = CHUNK
    m = np.zeros((2 + len(LEVELS), c, c), np.float32)
    for r in range(c):
        m[0, r, :r + 1] = 1.0
        m[1, r, r + 1:] = 1.0
        for l, b in enumerate(LEVELS):
            mid = (r // (2 * b)) * 2 * b + b - 1
            if r > mid:
                m[2 + l, r, mid + 1:r + 1] = 1.0
            else:
                m[2 + l, r, r + 1:mid + 1] = 1.0
    return m.reshape(-1, c)


def _level_masks():
    i = lax.broadcasted_iota(jnp.int32, (CHUNK, CHUNK), 0)
    j = lax.broadcasted_iota(jnp.int32, (CHUNK, CHUNK), 1)
    masks = []
    for b in LEVELS:
        same = (i // (2 * b)) == (j // (2 * b))
        masks.append(same & ((i % (2 * b)) >= b) & ((j % (2 * b)) < b))
    return masks


def _head_norm_gate(o, w, gate):
    return o * lax.rsqrt(jnp.mean(o * o, axis=-1, keepdims=True) + NORM_EPS) * w * jax.nn.silu(gate)


def _gla_chunks(n_heads, dk, dv, q_ref, k_ref, v_ref, gs_ref, rs_ref, o_ref, st_ref, n_chunks):
    n = n_heads * dk
    masks = _level_masks()

    def chunk(c, carry):
        r0 = pl.multiple_of(c * CHUNK, CHUNK)
        rows = pl.ds(r0, CHUNK)
        e3 = _dot(rs_ref[...], gs_ref[rows, :])
        x = jnp.exp((e3[:, 2 * n:3 * n] + e3[:, n:2 * n]) + e3[:, 0:n])
        qc, kc, vc = q_ref[rows, :], k_ref[rows, :], v_ref[rows, :]
        for h in range(n_heads):
            cols = slice(h * dk, (h + 1) * dk)
            qh, kh = qc[:, cols], kc[:, cols]
            vh = vc[:, h * dv:(h + 1) * dv]
            vb = vh.astype(BF16)
            st = st_ref[h]
            qd = (qh * x[0:CHUNK, cols]).astype(BF16)
            kd = (kh * x[CHUNK:2 * CHUNK, cols]).astype(BF16)
            a = jnp.zeros((CHUNK, CHUNK), F32)
            for l in range(len(LEVELS)):
                xl = x[(2 + l) * CHUNK:(3 + l) * CHUNK, cols]
                al = _dot_nt((qh * xl).astype(BF16), (kh * xl).astype(BF16))
                a = jnp.where(masks[l], al, a)
            diag = jnp.sum(qh * kh, axis=-1, keepdims=True)
            o = _dot_nt(qd, st.astype(BF16)) + _dot(a.astype(BF16), vb) + diag * vh
            o_ref[rows, h * dv:(h + 1) * dv] = o
            st_ref[h] = st * x[CHUNK - 1:CHUNK, cols] + _dot_tn(vb, kd)
        return carry

    lax.fori_loop(0, n_chunks, chunk, 0)


def _gla_kernel(mode, h_ref, nw_ref, w_ref, aux1_ref, aux2_ref, hw_ref, rs_ref, y_ref,
                p_ref, q_ref, k_ref, gs_ref, o_ref, st_ref):
    if mode == "hgrn":
        n_heads, dk, dv = HG_HEADS, HG_DH, HG_DH
    else:
        n_heads, dk, dv = GLA_HEADS, GLA_DK, GLA_DV
    n = n_heads * dk
    tile = h_ref.shape[0]

    @pl.when(pl.program_id(1) == 0)
    def _():
        st_ref[...] = jnp.zeros_like(st_ref)

    xn = _rms(h_ref[...], nw_ref[...]).astype(BF16)
    p_ref[...] = _dot(xn, w_ref[...])
    if mode == "hgrn":
        lb = aux1_ref[...]
        z = p_ref[:, n:2 * n]
        g = jnp.log(lb + (1.0 - lb) * jax.nn.sigmoid(z))
        q_ref[...] = p_ref[:, 0:n] * dk ** -0.5
        k_ref[...] = (1.0 - lb) * jax.nn.sigmoid(-z)
        v_off = 2 * n
    else:
        v_off = 2 * n
        glr = p_ref[:, v_off + 2 * MIX_W:v_off + 2 * MIX_W + LANE]
        logit = jnp.dot(glr, aux1_ref[...], preferred_element_type=F32, precision=HIGHEST) + aux2_ref[...]
        g = jax.nn.log_sigmoid(logit) / GLA_NORMALIZER
        q_ref[...] = p_ref[:, 0:n] * dk ** -0.5
        k_ref[...] = p_ref[:, n:2 * n]
    hi, mid, lo = _split3(g)
    gs_ref[:, 0:n] = hi
    gs_ref[:, n:2 * n] = mid
    gs_ref[:, 2 * n:3 * n] = lo

    _gla_chunks(n_heads, dk, dv, q_ref, k_ref, p_ref.at[:, v_off:v_off + MIX_W], gs_ref, rs_ref, o_ref, st_ref,
                tile // CHUNK)

    g_off = v_off + MIX_W
    for h in range(n_heads):
        cols = slice(h * dv, (h + 1) * dv)
        y = _head_norm_gate(o_ref[:, cols], hw_ref[...], p_ref[:, g_off + h * dv:g_off + (h + 1) * dv])
        y_ref[:, cols] = y.astype(y_ref.dtype)


def _gla_call(mode, h, norm_w, w, aux1, aux2, head_w, rs):
    bsz, seq, d = h.shape
    n_heads, dk, dv = (HG_HEADS, HG_DH, HG_DH) if mode == "hgrn" else (GLA_HEADS, GLA_DK, GLA_DV)
    n = n_heads * dk
    tile = min(MIX_TILE, seq)
    ncols = w.shape[1]
    full = lambda a: pl.BlockSpec(a.shape, lambda b, t: (0,) * a.ndim)
    return pl.pallas_call(
        functools.partial(_gla_kernel, mode),
        out_shape=jax.ShapeDtypeStruct((bsz, seq, MIX_W), BF16),
        grid=(bsz, seq // tile),
        in_specs=[pl.BlockSpec((None, tile, d), lambda b, t: (b, t, 0)),
                  full(norm_w), full(w), full(aux1), full(aux2), full(head_w), full(rs)],
        out_specs=pl.BlockSpec((None, tile, MIX_W), lambda b, t: (b, t, 0)),
        scratch_shapes=[pltpu.VMEM((tile, ncols), F32), pltpu.VMEM((tile, n), F32), pltpu.VMEM((tile, n), F32),
                        pltpu.VMEM((tile, 3 * n), BF16), pltpu.VMEM((tile, MIX_W), F32),
                        pltpu.VMEM((n_heads, dv, dk), F32)],
        compiler_params=_cparams(2),
        name="mix_" + mode,
    )(h, norm_w, w, aux1, aux2, head_w, rs)


def _delta_kernel(h_ref, nw_ref, w_ref, cw_ref, alog_ref, dtb_ref, hw_ref, tril_ref, y_ref,
                  p_ref, xs_ref, q_ref, k_ref, v_ref, sm_ref, o_ref, st_ref):
    tile = h_ref.shape[0]
    nh, dh = DN_HEADS, DN_DH
    qkv = 3 * MIX_W

    @pl.when(pl.program_id(1) == 0)
    def _():
        st_ref[...] = jnp.zeros_like(st_ref)
        xs_ref[0:SUBLANE, :] = jnp.zeros((SUBLANE, qkv), F32)

    xn = _rms(h_ref[...], nw_ref[...]).astype(BF16)
    p_ref[...] = _dot(xn, w_ref[...])

    xs_ref[SUBLANE:SUBLANE + tile, :] = p_ref[:, 0:qkv]
    conv = cw_ref[CONV_K - 1:CONV_K, :] * xs_ref[SUBLANE:SUBLANE + tile, :]
    for j in range(1, CONV_K):
        conv = conv + cw_ref[CONV_K - 1 - j:CONV_K - j, :] * xs_ref[SUBLANE - j:SUBLANE - j + tile, :]
    xs_ref[0:SUBLANE, :] = xs_ref[tile:tile + SUBLANE, :]
    conv = jax.nn.silu(conv)
    for h in range(nh):
        cols = slice(h * dh, (h + 1) * dh)
        qh = conv[:, h * dh:(h + 1) * dh]
        kh = conv[:, MIX_W + h * dh:MIX_W + (h + 1) * dh]
        q_ref[:, cols] = qh * lax.rsqrt(jnp.sum(qh * qh, axis=-1, keepdims=True) + NORM_EPS) * dh ** -0.5
        k_ref[:, cols] = kh * lax.rsqrt(jnp.sum(kh * kh, axis=-1, keepdims=True) + NORM_EPS)
    v_ref[...] = conv[:, 2 * MIX_W:3 * MIX_W]

    small = p_ref[:, qkv + MIX_W:qkv + MIX_W + LANE]
    beta_all = jax.nn.sigmoid(small)
    gdec = -jnp.exp(alog_ref[...]) * jax.nn.softplus(small + dtb_ref[...])
    hi, mid, lo = _split3(gdec)
    sm_ref[:, 0:LANE] = hi
    sm_ref[:, LANE:2 * LANE] = mid
    sm_ref[:, 2 * LANE:3 * LANE] = lo

    ii = lax.broadcasted_iota(jnp.int32, (CHUNK, CHUNK), 0)
    jj = lax.broadcasted_iota(jnp.int32, (CHUNK, CHUNK), 1)
    incl = ii >= jj
    strict = ii > jj
    eye = (ii == jj).astype(F32)

    def chunk(c, carry):
        r0 = pl.multiple_of(c * CHUNK, CHUNK)
        rows = pl.ds(r0, CHUNK)
        g3 = _dot(tril_ref[...], sm_ref[rows, :])
        gcum = (g3[:, 2 * LANE:3 * LANE] + g3[:, LANE:2 * LANE]) + g3[:, 0:LANE]
        gcum_t = gcum.T
        beta_c = jax.nn.sigmoid(p_ref[rows, qkv + MIX_W:qkv + MIX_W + LANE])
        qc, kc, vc = q_ref[rows, :], k_ref[rows, :], v_ref[rows, :]
        for h in range(nh):
            cols = slice(h * dh, (h + 1) * dh)
            qh, kh, vh = qc[:, cols], kc[:, cols], vc[:, cols]
            gcol = gcum[:, nh + h:nh + h + 1]
            grow = gcum_t[nh + h:nh + h + 1, :]
            glast = gcum[CHUNK - 1:CHUNK, nh + h:nh + h + 1]
            bh = beta_c[:, h:h + 1]
            decay = jnp.exp(jnp.where(incl, gcol - grow, -jnp.inf))
            kb = kh * bh
            kbb, khb = kb.astype(BF16), kh.astype(BF16)
            lower = jnp.where(strict, _dot_nt(kbb, khb) * decay, 0.0)
            t_inv = eye - lower
            pw = jnp.dot(lower, lower, preferred_element_type=F32, precision=HIGHEST)
            for it in range(5):
                t_inv = t_inv + jnp.dot(t_inv, pw, preferred_element_type=F32, precision=HIGHEST)
                if it < 4:
                    pw = jnp.dot(pw, pw, preferred_element_type=F32, precision=HIGHEST)
            tb = t_inv.astype(BF16)
            egc = jnp.exp(gcol)
            u = _dot(tb, (vh * bh).astype(BF16))
            w = _dot(tb, (kb * egc).astype(BF16))
            qk = _dot_nt(qh.astype(BF16), khb) * decay
            st = st_ref[h]
            sb = st.astype(BF16)
            v_new = u - _dot(w.astype(BF16), sb)
            vnb = v_new.astype(BF16)
            o = _dot((qh * egc).astype(BF16), sb) + _dot(qk.astype(BF16), vnb)
            o_ref[rows, cols] = o
            k_dec = (kh * jnp.exp(glast - gcol)).astype(BF16)
            st_ref[h] = st * jnp.exp(glast) + _dot_tn(k_dec, vnb)
        return carry

    del beta_all
    lax.fori_loop(0, tile // CHUNK, chunk, 0)

    g_off = qkv
    for h in range(nh):
        cols = slice(h * dh, (h + 1) * dh)
        y = _head_norm_gate(o_ref[:, cols], hw_ref[...], p_ref[:, g_off + h * dh:g_off + (h + 1) * dh])
        y_ref[:, cols] = y.astype(y_ref.dtype)


def _delta_call(h, norm_w, w, conv_w, alog_row, dtb_row, head_w, tril):
    bsz, seq, d = h.shape
    tile = min(MIX_TILE, seq)
    ncols = w.shape[1]
    full = lambda a: pl.BlockSpec(a.shape, lambda b, t: (0,) * a.ndim)
    return pl.pallas_call(
        _delta_kernel,
        out_shape=jax.ShapeDtypeStruct((bsz, seq, MIX_W), BF16),
        grid=(bsz, seq // tile),
        in_specs=[pl.BlockSpec((None, tile, d), lambda b, t: (b, t, 0)),
                  full(norm_w), full(w), full(conv_w), full(alog_row), full(dtb_row), full(head_w), full(tril)],
        out_specs=pl.BlockSpec((None, tile, MIX_W), lambda b, t: (b, t, 0)),
        scratch_shapes=[pltpu.VMEM((tile, ncols), F32), pltpu.VMEM((tile + SUBLANE, 3 * MIX_W), F32),
                        pltpu.VMEM((tile, MIX_W), F32), pltpu.VMEM((tile, MIX_W), F32),
                        pltpu.VMEM((tile, MIX_W), F32), pltpu.VMEM((tile, 3 * LANE), BF16),
                        pltpu.VMEM((tile, MIX_W), F32), pltpu.VMEM((DN_HEADS, DN_DH, DN_DH), F32)],
        compiler_params=_cparams(2),
        name="mix_delta",
    )(h, norm_w, w, conv_w, alog_row, dtb_row, head_w, tril)


def _s5_proj_kernel(h_ref, nw_ref, w_ref, o_ref):
    xn = _rms(h_ref[...], nw_ref[...]).astype(BF16)
    o_ref[...] = _dot(xn, w_ref[...])


def _s5_proj_call(h, norm_w, w):
    bsz, seq, d = h.shape
    tile = min(ROW_TILE, seq)
    return pl.pallas_call(
        _s5_proj_kernel,
        out_shape=jax.ShapeDtypeStruct((seq, bsz * MIX_W), F32),
        grid=(bsz, seq // tile),
        in_specs=[pl.BlockSpec((None, tile, d), lambda b, t: (b, t, 0)),
                  pl.BlockSpec(norm_w.shape, lambda b, t: (0, 0)),
                  pl.BlockSpec(w.shape, lambda b, t: (0, 0))],
        out_specs=pl.BlockSpec((tile, MIX_W), lambda b, t: (t, b)),
        compiler_params=_cparams(2),
        name="s5_proj",
    )(h, norm_w, w)


def _s5_kernel(u_ref, bre_ref, bim_ref, ar_ref, ai_ref, cre_ref, cim_ref, d_ref, gw_ref, gb_ref, y_ref,
               xr_ref, xi_ref, sr_ref, si_ref):
    rows = u_ref.shape[0]
    bsz = sr_ref.shape[0]
    steps = rows // bsz
    n_cl = bre_ref.shape[0]
    cw = bre_ref.shape[1]
    sw = bre_ref.shape[2]

    @pl.when(pl.program_id(0) == 0)
    def _():
        sr_ref[...] = jnp.zeros_like(sr_ref)
        si_ref[...] = jnp.zeros_like(si_ref)

    u = u_ref[...]
    ub = u.astype(BF16)
    for c in range(n_cl):
        uc = ub[:, c * cw:(c + 1) * cw]
        xr_ref[:, c * sw:(c + 1) * sw] = _dot(uc, bre_ref[c])
        xi_ref[:, c * sw:(c + 1) * sw] = _dot(uc, bim_ref[c])

    for c in range(n_cl):
        ls = slice(c * sw, (c + 1) * sw)
        ar, ai = ar_ref[:, ls], ai_ref[:, ls]

        def step(i, carry, ls=ls, ar=ar, ai=ai):
            xr, xi = carry
            rr = pl.ds(pl.multiple_of(i * bsz, bsz), bsz)
            nxr = ar * xr - ai * xi + xr_ref[rr, ls]
            nxi = ar * xi + ai * xr + xi_ref[rr, ls]
            xr_ref[rr, ls] = nxr
            xi_ref[rr, ls] = nxi
            return nxr, nxi

        xr, xi = lax.fori_loop(0, steps, step, (sr_ref[:, ls], si_ref[:, ls]), unroll=8)
        sr_ref[:, ls] = xr
        si_ref[:, ls] = xi

    ys = []
    for c in range(n_cl):
        ls = slice(c * sw, (c + 1) * sw)
        ys.append(_dot(xr_ref[:, ls].astype(BF16), cre_ref[c]) - _dot(xi_ref[:, ls].astype(BF16), cim_ref[c]))
    y = jnp.concatenate(ys, axis=1) + d_ref[...] * u
    z = jax.nn.gelu(y)
    gl = _dot(z.astype(BF16), gw_ref[...]) + gb_ref[...]
    y_ref[...] = (z * jax.nn.sigmoid(gl)).astype(y_ref.dtype)


def _s5_call(u_tm, bsz, bre, bim, ar, ai, cre, cim, d_row, glu_w, glu_b):
    n_rows = u_tm.shape[0]
    seq = n_rows // bsz
    steps = min(S5_STEPS, seq)
    rows = steps * bsz
    n_state = ar.shape[1]
    full = lambda a: pl.BlockSpec(a.shape, lambda t: (0,) * a.ndim)
    return pl.pallas_call(
        _s5_kernel,
        out_shape=jax.ShapeDtypeStruct((n_rows, MIX_W), BF16),
        grid=(seq // steps,),
        in_specs=[pl.BlockSpec((rows, MIX_W), lambda t: (t, 0)),
                  full(bre), full(bim), full(ar), full(ai), full(cre), full(cim), full(d_row), full(glu_w),
                  full(glu_b)],
        out_specs=pl.BlockSpec((rows, MIX_W), lambda t: (t, 0)),
        scratch_shapes=[pltpu.VMEM((rows, n_state), F32), pltpu.VMEM((rows, n_state), F32),
                        pltpu.VMEM((bsz, n_state), F32), pltpu.VMEM((bsz, n_state), F32)],
        compiler_params=_cparams(1),
        name="mix_s5",
    )(u_tm, bre, bim, ar, ai, cre, cim, d_row, glu_w, glu_b)


def _s5_params(a_re, a_im, log_dt, b_re, b_im, c_re, c_im, bsz):
    dt = jnp.exp(log_dt)[:, None]
    mag = jnp.exp(a_re * dt)
    abar_r, abar_i = mag * jnp.cos(a_im * dt), mag * jnp.sin(a_im * dt)
    den = a_re * a_re + a_im * a_im
    nr, ni = abar_r - 1.0, abar_i
    zr, zi = (nr * a_re + ni * a_im) / den, (ni * a_re - nr * a_im) / den
    bbar_r = zr[..., None] * b_re - zi[..., None] * b_im
    bbar_i = zr[..., None] * b_im + zi[..., None] * b_re
    n_cl = S5_GROUPS // S5_CLUSTER
    eye = jnp.eye(S5_CLUSTER, dtype=F32)

    def pack_b(bb):
        bb = bb.reshape(n_cl, S5_CLUSTER, S5_STATE, S5_GROUP)
        out = jnp.einsum("cgph,gk->cghkp", bb, eye)
        return out.reshape(n_cl, S5_CLUSTER * S5_GROUP, S5_CLUSTER * S5_STATE).astype(BF16)

    def pack_c(cc):
        cc = cc.reshape(n_cl, S5_CLUSTER, S5_GROUP, S5_STATE)
        out = jnp.einsum("cghp,gk->cgpkh", cc, eye)
        return out.reshape(n_cl, S5_CLUSTER * S5_STATE, S5_CLUSTER * S5_GROUP).astype(BF16)

    bc = lambda a: jnp.broadcast_to(a.reshape(1, -1), (bsz, S5_GROUPS * S5_STATE))
    return pack_b(bbar_r), pack_b(bbar_i), bc(abar_r), bc(abar_i), pack_c(c_re), pack_c(c_im)


def _merge_kernel(h_ref, nw_ref, wm_ref, ya_ref, yb_ref, yc_ref, yd_ref, wb_ref, wo_ref, o_ref):
    h = h_ref[...]
    xn = _rms(h, nw_ref[...]).astype(BF16)
    d = h.shape[1]
    mixed = jnp.zeros(h.shape, F32)
    for n, y_ref in enumerate((ya_ref, yb_ref, yc_ref, yd_ref)):
        gate = jax.nn.sigmoid(_dot(xn, wm_ref[:, n * d:(n + 1) * d]))
        mixed = mixed + gate * _dot(y_ref[...], wb_ref[n])
    o_ref[...] = h + _dot(mixed.astype(BF16), wo_ref[...])


def _merge_call(h, norm_w, w_m, y_a, y_b_tm, y_c, y_d, w_b, w_o):
    bsz, seq, d = h.shape
    tile = min(ROW_TILE, seq)
    full = lambda a: pl.BlockSpec(a.shape, lambda b, t: (0,) * a.ndim)
    ytile = pl.BlockSpec((None, tile, MIX_W), lambda b, t: (b, t, 0))
    return pl.pallas_call(
        _merge_kernel,
        out_shape=jax.ShapeDtypeStruct(h.shape, F32),
        grid=(bsz, seq // tile),
        in_specs=[pl.BlockSpec((None, tile, d), lambda b, t: (b, t, 0)), full(norm_w), full(w_m),
                  ytile, pl.BlockSpec((tile, MIX_W), lambda b, t: (t, b)), ytile, ytile, full(w_b), full(w_o)],
        out_specs=pl.BlockSpec((None, tile, d), lambda b, t: (b, t, 0)),
        compiler_params=_cparams(2),
        name="merge",
    )(h, norm_w, w_m, y_a, y_b_tm, y_c, y_d, w_b, w_o)


def _ffn_kernel(h_ref, nw_ref, wg_ref, wu_ref, wd_ref, o_ref, hn_ref, acc_ref):
    j = pl.program_id(1)

    @pl.when(j == 0)
    def _():
        hn_ref[...] = _rms(h_ref[...], nw_ref[...]).astype(BF16)
        acc_ref[...] = jnp.zeros_like(acc_ref)

    hn = hn_ref[...]
    act = (jax.nn.silu(_dot(hn, wg_ref[...])) * _dot(hn, wu_ref[...])).astype(BF16)
    acc_ref[...] += _dot(act, wd_ref[...])

    @pl.when(j == pl.num_programs(1) - 1)
    def _():
        o_ref[...] = h_ref[...] + acc_ref[...]


def _ffn_call(h2, norm_w, w_gate, w_up, w_down, f_tiles):
    n_tok, d = h2.shape
    f = w_gate.shape[1]
    tf = f // f_tiles
    tile = min(ROW_TILE, n_tok)
    return pl.pallas_call(
        _ffn_kernel,
        out_shape=jax.ShapeDtypeStruct(h2.shape, F32),
        grid=(n_tok // tile, f_tiles),
        in_specs=[pl.BlockSpec((tile, d), lambda i, j: (i, 0)),
                  pl.BlockSpec(norm_w.shape, lambda i, j: (0, 0)),
                  pl.BlockSpec((d, tf), lambda i, j: (0, j)),
                  pl.BlockSpec((d, tf), lambda i, j: (0, j)),
                  pl.BlockSpec((tf, d), lambda i, j: (j, 0))],
        out_specs=pl.BlockSpec((tile, d), lambda i, j: (i, 0)),
        scratch_shapes=[pltpu.VMEM((tile, d), BF16), pltpu.VMEM((tile, d), F32)],
        compiler_params=_cparams(2),
        name="ffn_dense",
    )(h2, norm_w, w_gate, w_up, w_down)


def _route_kernel(h_ref, nw_ref, rt_ref, hn_ref, idx_ref, gate_ref):
    hn = _rms(h_ref[...], nw_ref[...])
    hn_ref[...] = hn.astype(BF16)
    logits = jnp.dot(hn, rt_ref[...], preferred_element_type=F32, precision=HIGHEST)
    lg = logits.T[0:N_EXPERTS, :]
    row = lax.broadcasted_iota(jnp.int32, lg.shape, 0)
    m1 = jnp.max(lg, axis=0, keepdims=True)
    i1 = jnp.min(jnp.where(lg == m1, row, N_EXPERTS), axis=0, keepdims=True)
    lg2 = jnp.where(row == i1, -jnp.inf, lg)
    m2 = jnp.max(lg2, axis=0, keepdims=True)
    i2 = jnp.min(jnp.where(lg2 == m2, row, N_EXPERTS), axis=0, keepdims=True)
    e2 = jnp.exp(m2 - m1)
    g1 = 1.0 / (1.0 + e2)
    g2 = e2 / (1.0 + e2)
    idx_ref[...] = jnp.where(row == 0, i1, jnp.where(row == 1, i2, 0))
    gate_ref[...] = jnp.where(row == 0, g1, jnp.where(row == 1, g2, 0.0))


def _route_call(h2, norm_w, router_pad):
    n_tok, d = h2.shape
    tile = min(ROW_TILE, n_tok)
    return pl.pallas_call(
        _route_kernel,
        out_shape=(jax.ShapeDtypeStruct((n_tok, d), BF16),
                   jax.ShapeDtypeStruct((N_EXPERTS, n_tok), jnp.int32),
                   jax.ShapeDtypeStruct((N_EXPERTS, n_tok), F32)),
        grid=(n_tok // tile,),
        in_specs=[pl.BlockSpec((tile, d), lambda i: (i, 0)),
                  pl.BlockSpec(norm_w.shape, lambda i: (0, 0)),
                  pl.BlockSpec(router_pad.shape, lambda i: (0, 0))],
        out_specs=(pl.BlockSpec((tile, d), lambda i: (i, 0)),
                   pl.BlockSpec((N_EXPERTS, tile), lambda i: (0, i)),
                   pl.BlockSpec((N_EXPERTS, tile), lambda i: (0, i))),
        compiler_params=_cparams(1),
        name="moe_route",
    )(h2, norm_w, router_pad)


def _expert_kernel(be_ref, bv_ref, x_ref, wg_ref, wu_ref, wd_ref, o_ref, acc_ref):
    i, j = pl.program_id(0), pl.program_id(1)
    last = pl.num_programs(1) - 1

    @pl.when(j == 0)
    def _():
        acc_ref[...] = jnp.zeros_like(acc_ref)

    @pl.when(bv_ref[i] != 0)
    def _():
        x = x_ref[...]
        act = (jax.nn.silu(_dot(x, wg_ref[...])) * _dot(x, wu_ref[...])).astype(BF16)
        acc_ref[...] += _dot(act, wd_ref[...])

    @pl.when(j == last)
    def _():
        o_ref[...] = acc_ref[...]


def _expert_call(blk_e, blk_valid, xs, w_gate, w_up, w_down, f_tiles):
    n_rows, d = xs.shape
    f = w_gate.shape[2]
    tf = f // f_tiles
    n_blk = n_rows // MOE_TILE
    grid_spec = pltpu.PrefetchScalarGridSpec(
        num_scalar_prefetch=2,
        grid=(n_blk, f_tiles),
        in_specs=[pl.BlockSpec((MOE_TILE, d), lambda i, j, be, bv: (i, 0)),
                  pl.BlockSpec((None, d, tf), lambda i, j, be, bv: (be[i], 0, j)),
                  pl.BlockSpec((None, d, tf), lambda i, j, be, bv: (be[i], 0, j)),
                  pl.BlockSpec((None, tf, d), lambda i, j, be, bv: (be[i], j, 0))],
        out_specs=pl.BlockSpec((MOE_TILE, d), lambda i, j, be, bv: (i, 0)),
        scratch_shapes=[pltpu.VMEM((MOE_TILE, d), F32)],
    )
    return pl.pallas_call(
        _expert_kernel,
        out_shape=jax.ShapeDtypeStruct((n_rows, d), F32),
        grid_spec=grid_spec,
        compiler_params=_cparams(2),
        name="moe_experts",
    )(blk_e, blk_valid, xs, w_gate, w_up, w_down)


def _combine_kernel(apply_norm, h_ref, y0_ref, y1_ref, g0_ref, g1_ref, nw_ref, o_ref):
    out = h_ref[...] + (g0_ref[...] * y0_ref[...] + g1_ref[...] * y1_ref[...])
    if apply_norm:
        out = _rms(out, nw_ref[...])
    o_ref[...] = out


def _combine_call(h2, y0, y1, g0, g1, norm_w, apply_norm):
    n_tok, d = h2.shape
    tile = min(ROW_TILE, n_tok)
    row = pl.BlockSpec((tile, d), lambda i: (i, 0))
    col = pl.BlockSpec((tile, 1), lambda i: (i, 0))
    return pl.pallas_call(
        functools.partial(_combine_kernel, apply_norm),
        out_shape=jax.ShapeDtypeStruct(h2.shape, F32),
        grid=(n_tok // tile,),
        in_specs=[row, row, row, col, col, pl.BlockSpec(norm_w.shape, lambda i: (0, 0))],
        out_specs=row,
        compiler_params=_cparams(1),
        name="moe_combine",
    )(h2, y0, y1, g0, g1, norm_w)


def _final_norm_kernel(h_ref, nw_ref, o_ref):
    o_ref[...] = _rms(h_ref[...], nw_ref[...])


def _final_norm_call(h2, norm_w):
    n_tok, d = h2.shape
    tile = min(ROW_TILE, n_tok)
    row = pl.BlockSpec((tile, d), lambda i: (i, 0))
    return pl.pallas_call(
        _final_norm_kernel,
        out_shape=jax.ShapeDtypeStruct(h2.shape, F32),
        grid=(n_tok // tile,),
        in_specs=[row, pl.BlockSpec(norm_w.shape, lambda i: (0, 0))],
        out_specs=row,
        compiler_params=_cparams(1),
        name="final_norm",
    )(h2, norm_w)


def _moe(h2, norm_w, router, w_gate, w_up, w_down, final_w, apply_final):
    n_tok, d = h2.shape
    router_pad = jnp.zeros((d, LANE), F32).at[:, :N_EXPERTS].set(router)
    hn, idx, gate = _route_call(h2, norm_w, router_pad)
    e_flat = idx[:TOP_K].reshape(-1)
    tok_flat = jnp.tile(jnp.arange(n_tok, dtype=jnp.int32), TOP_K)
    onehot = (e_flat[:, None] == jnp.arange(N_EXPERTS, dtype=jnp.int32)[None, :]).astype(jnp.int32)
    csum = jnp.cumsum(onehot, axis=0)
    rank = jnp.sum(csum * onehot, axis=1) - 1
    counts = csum[-1]
    padded = (counts + MOE_TILE - 1) // MOE_TILE * MOE_TILE
    pad_end = jnp.cumsum(padded)
    pad_start = pad_end - padded
    dest = pad_start[e_flat] + rank
    n_assign = n_tok * TOP_K
    n_blk = -(-n_assign // MOE_TILE) + N_EXPERTS
    src_tok = jnp.zeros((n_blk * MOE_TILE,), jnp.int32).at[dest].set(tok_flat)
    blk_start = jnp.arange(n_blk, dtype=jnp.int32) * MOE_TILE
    blk_e = jnp.minimum(jnp.searchsorted(pad_end, blk_start, side="right"), N_EXPERTS - 1).astype(jnp.int32)
    blk_valid = (blk_start < pad_end[-1]).astype(jnp.int32)
    xs = jnp.take(hn, src_tok, axis=0)
    f_tiles = 2 if w_gate.shape[2] % (2 * LANE) == 0 else 1
    ys = _expert_call(blk_e, blk_valid, xs, w_gate, w_up, w_down, f_tiles)
    y0 = jnp.take(ys, dest[:n_tok], axis=0)
    y1 = jnp.take(ys, dest[n_tok:], axis=0)
    g0 = gate[0].reshape(n_tok, 1)
    g1 = gate[1].reshape(n_tok, 1)
    return _combine_call(h2, y0, y1, g0, g1, final_w, apply_final)


def _pad_cols(w, n):
    return jnp.concatenate([w, jnp.zeros((w.shape[0], n - w.shape[1]), w.dtype)], axis=1)


def _token_mix(h, norm_w, w_in, lb, hgrn_norm, s5, s5_d, s5_glu_w, s5_glu_b, dn_conv_w, dn_a_log, dn_dt_bias,
               dn_norm, gla_w_lr, gla_b_lr, gla_norm, w_branch, w_out, rs, tril):
    bsz, seq, d = h.shape
    m = MIX_W
    o = 0
    w_a = w_in[:, o:o + 4 * m]; o += 4 * m
    w_s5 = w_in[:, o:o + m]; o += m
    w_qkv = w_in[:, o:o + 3 * m]; o += 3 * m
    w_bd = w_in[:, o:o + 2 * DN_HEADS]; o += 2 * DN_HEADS
    w_dg = w_in[:, o:o + m]; o += m
    nqk = GLA_HEADS * GLA_DK
    w_gqk = w_in[:, o:o + 2 * nqk]; o += 2 * nqk
    w_gv = w_in[:, o:o + m]; o += m
    w_glr = w_in[:, o:o + GLA_RANK]; o += GLA_RANK
    w_gg = w_in[:, o:o + m]; o += m
    w_m = w_in[:, o:o + N_BRANCH * d]

    w_a = w_a.astype(BF16)
    w_c = jnp.concatenate([w_qkv, w_dg, _pad_cols(w_bd, LANE)], axis=1).astype(BF16)
    w_d = jnp.concatenate([w_gqk, w_gv, w_gg, _pad_cols(w_glr, LANE)], axis=1).astype(BF16)
    row = lambda a: a.reshape(1, -1).astype(F32)
    nw = row(norm_w)

    dummy = jnp.zeros((1, LANE), F32)
    y_a = _gla_call("hgrn", h, nw, w_a, row(lb), dummy, row(hgrn_norm), rs)

    wlr_pad = jnp.zeros((LANE, nqk), F32).at[:GLA_RANK].set(gla_w_lr)
    y_d = _gla_call("gla", h, nw, w_d, wlr_pad, row(gla_b_lr), row(gla_norm), rs)

    lane_row = lambda a: jnp.zeros((1, LANE), F32).at[0, DN_HEADS:2 * DN_HEADS].set(a)
    y_c = _delta_call(h, nw, w_c, dn_conv_w.astype(F32), lane_row(dn_a_log), lane_row(dn_dt_bias), row(dn_norm), tril)

    u_tm = _s5_proj_call(h, nw, w_s5.astype(BF16)).reshape(seq * bsz, m)
    bre, bim, ar, ai, cre, cim = s5
    y_b = _s5_call(u_tm, bsz, bre, bim, ar, ai, cre, cim, row(s5_d), s5_glu_w.astype(BF16), row(s5_glu_b))
    y_b = y_b.reshape(seq, bsz * m)

    return _merge_call(h, nw, w_m.astype(BF16), y_a, y_b, y_c, y_d, w_branch.astype(BF16), w_out.astype(BF16))


def kernel(x, attn_norm, ffn_norm, final_norm, w_in, hgrn_lb, hgrn_norm, s5_a_re, s5_a_im, s5_log_dt, s5_b_re, s5_b_im, s5_c_re, s5_c_im, s5_d, s5_glu_w, s5_glu_b, dn_conv_w, dn_a_log, dn_dt_bias, dn_norm, gla_w_lr, gla_b_lr, gla_norm, w_branch, w_out, ffn_w_gate, ffn_w_up, ffn_w_down, moe_router, moe_w_gate, moe_w_up, moe_w_down):
    bsz, seq, d = x.shape
    depth = w_in.shape[0]
    assert bsz == SUBLANE, "the S5 kernel keeps one batch row per sublane"
    assert seq % CHUNK == 0 and d == D_MODEL
    lb_soft = jax.nn.softmax(hgrn_lb.astype(F32), axis=0)
    lb_all = jnp.cumsum(lb_soft, axis=0) - lb_soft[0]
    rs = jnp.asarray(_range_sum_matrix(), BF16)
    tril = jnp.asarray(np.tril(np.ones((CHUNK, CHUNK), np.float32)), BF16)
    row = lambda a: a.reshape(1, -1).astype(F32)

    h = x
    for layer in range(depth):
        s5 = _s5_params(s5_a_re[layer], s5_a_im[layer], s5_log_dt[layer], s5_b_re[layer], s5_b_im[layer],
                        s5_c_re[layer], s5_c_im[layer], bsz)
        h = _token_mix(h, attn_norm[layer], w_in[layer], lb_all[layer], hgrn_norm[layer], s5, s5_d[layer],
                       s5_glu_w[layer], s5_glu_b[layer], dn_conv_w[layer], dn_a_log[layer], dn_dt_bias[layer],
                       dn_norm[layer], gla_w_lr[layer], gla_b_lr[layer], gla_norm[layer], w_branch[layer],
                       w_out[layer], rs, tril)
        h2 = h.reshape(bsz * seq, d)
        j = layer // 2
        last = layer == depth - 1
        if layer % 2 == 0:
            f = ffn_w_gate.shape[2]
            f_tiles = 2 if f % (2 * LANE) == 0 else 1
            h2 = _ffn_call(h2, row(ffn_norm[layer]), ffn_w_gate[j].astype(BF16), ffn_w_up[j].astype(BF16),
                           ffn_w_down[j].astype(BF16), f_tiles)
            if last:
                h2 = _final_norm_call(h2, row(final_norm))
        else:
            h2 = _moe(h2, row(ffn_norm[layer]), moe_router[j], moe_w_gate[j].astype(BF16),
                      moe_w_up[j].astype(BF16), moe_w_down[j].astype(BF16), row(final_norm), last)
        h = h2.reshape(bsz, seq, d)
    return h
```

```python
import functools
import math

import numpy as np
import jax
import jax.numpy as jnp
from jax import lax
from jax.experimental import pallas as pl
from jax.experimental.pallas import tpu as pltpu

F32 = jnp.float32
BF16 = jnp.bfloat16
HIGHEST = lax.Precision.HIGHEST

D_MODEL = 1024
MIX_W = 512
HG_HEADS = 4
HG_DH = MIX_W // HG_HEADS
S5_GROUP = 16
S5_GROUPS = MIX_W // S5_GROUP
S5_STATE = 64
DN_HEADS = 4
DN_DH = MIX_W // DN_HEADS
CONV_K = 4
GLA_HEADS = 4
GLA_DK = 64
GLA_DV = MIX_W // GLA_HEADS
GLA_RANK = 16
GLA_NORMALIZER = 16.0
CHUNK = 64
N_BRANCH = 4
N_EXPERTS = 8
TOP_K = 2
NORM_EPS = 1e-6

LANE = 128
SUBLANE = 8
VMEM_LIMIT = 56 * 1024 * 1024

MIX_TILE = 256
S5_STEPS = 64
S5_CLUSTER = 8
ROW_TILE = 512
MOE_TILE = 512
LEVELS = (32, 16, 8, 4, 2, 1)


def _cparams(n_axes):
    return pltpu.CompilerParams(dimension_semantics=("arbitrary",) * n_axes, vmem_limit_bytes=VMEM_LIMIT)


def _rms(x, w):
    return x * lax.rsqrt(jnp.mean(x * x, axis=-1, keepdims=True) + NORM_EPS) * w


def _dot(a, b):
    return jnp.dot(a, b, preferred_element_type=F32)


def _dot_nt(a, b, precision=None):
    return lax.dot_general(a, b, (((1,), (1,)), ((), ())), preferred_element_type=F32, precision=precision)


def _dot_tn(a, b):
    return lax.dot_general(a, b, (((0,), (0,)), ((), ())), preferred_element_type=F32)


def _split3(x):
    hi = x.astype(BF16)
    r1 = x - hi.astype(F32)
    mid = r1.astype(BF16)
    lo = (r1 - mid.astype(F32)).astype(BF16)
    return hi, mid, lo


def _split2(x):
    hi = x.astype(BF16)
    return hi, (x - hi.astype(F32)).astype(BF16)


def _lhs3(hi, lo):
    return jnp.concatenate([hi, hi, lo], axis=1)


def _rhs3(hi, lo):
    return jnp.concatenate([hi, lo, hi], axis=0)


def _mm3(lhs3, rhs3):
    return jnp.dot(lhs3, rhs3, preferred_element_type=F32)


def _range_sum_matrix():
    c = CHUNK
    m = np.zeros((2 + len(LEVELS), c, c), np.float32)
    for r in range(c):
        m[0, r, :r + 1] = 1.0
        m[1, r, r + 1:] = 1.0
        for l, b in enumerate(LEVELS):
            mid = (r // (2 * b)) * 2 * b + b - 1
            if r > mid:
                m[2 + l, r, mid + 1:r + 1] = 1.0
            else:
                m[2 + l, r, r + 1:mid + 1] = 1.0
    m = m.reshape(-1, c)
    return np.concatenate([m, m, m], axis=1)


def _level_masks():
    i = lax.broadcasted_iota(jnp.int32, (CHUNK, CHUNK), 0)
    j = lax.broadcasted_iota(jnp.int32, (CHUNK, CHUNK), 1)
    masks = []
    for b in LEVELS:
        same = (i // (2 * b)) == (j // (2 * b))
        masks.append(same & ((i % (2 * b)) >= b) & ((j % (2 * b)) < b))
    return masks


def _head_norm_gate(o, w, gate):
    return o * lax.rsqrt(jnp.mean(o * o, axis=-1, keepdims=True) + NORM_EPS) * w * jax.nn.silu(gate)


def _gla_chunks(n_heads, dk, dv, q_ref, k_ref, v_ref, gs_ref, rs_ref, o_ref, st_ref, n_chunks):
    masks = _level_masks()

    pre = []
    for c in range(n_chunks):
        rows = slice(c * CHUNK, (c + 1) * CHUNK)
        g3 = jnp.concatenate([gs_ref[0, rows, :], gs_ref[1, rows, :], gs_ref[2, rows, :]], axis=0)
        x = jnp.exp(_dot(rs_ref[...], g3))
        qc, kc, vc = q_ref[rows, :], k_ref[rows, :], v_ref[rows, :]
        heads = []
        for h in range(n_heads):
            cols = slice(h * dk, (h + 1) * dk)
            qh, kh = qc[:, cols], kc[:, cols]
            vh = vc[:, h * dv:(h + 1) * dv]
            vb = vh.astype(BF16)
            qd = (qh * x[0:CHUNK, cols]).astype(BF16)
            kd = (kh * x[CHUNK:2 * CHUNK, cols]).astype(BF16)
            a = jnp.zeros((CHUNK, CHUNK), F32)
            for l in range(len(LEVELS)):
                xl = x[(2 + l) * CHUNK:(3 + l) * CHUNK, cols]
                al = _dot_nt((qh * xl).astype(BF16), (kh * xl).astype(BF16))
                a = jnp.where(masks[l], al, a)
            diag = jnp.sum(qh * kh, axis=-1, keepdims=True)
            o_ref[rows, h * dv:(h + 1) * dv] = _dot(a.astype(BF16), vb) + diag * vh
            heads.append((qd, kd, vb, x[CHUNK - 1:CHUNK, cols]))
        pre.append(heads)

    st = [st_ref[h] for h in range(n_heads)]
    for c in range(n_chunks):
        rows = slice(c * CHUNK, (c + 1) * CHUNK)
        for h in range(n_heads):
            qd, kd, vb, dec = pre[c][h]
            o_ref[rows, h * dv:(h + 1) * dv] += _dot_nt(qd, st[h].astype(BF16))
            st[h] = st[h] * dec + _dot_tn(vb, kd)
    for h in range(n_heads):
        st_ref[h] = st[h]


def _gla_kernel(mode, h_ref, nw_ref, w_ref, aux1_ref, aux2_ref, hw_ref, rs_ref, y_ref,
                p_ref, q_ref, k_ref, gs_ref, o_ref, st_ref):
    if mode == "hgrn":
        n_heads, dk, dv = HG_HEADS, HG_DH, HG_DH
    else:
        n_heads, dk, dv = GLA_HEADS, GLA_DK, GLA_DV
    n = n_heads * dk
    tile = h_ref.shape[0]

    @pl.when(pl.program_id(1) == 0)
    def _():
        st_ref[...] = jnp.zeros_like(st_ref)

    xn = _rms(h_ref[...], nw_ref[...]).astype(BF16)
    p_ref[...] = _dot(xn, w_ref[...])
    if mode == "hgrn":
        lb = aux1_ref[...]
        z = p_ref[:, n:2 * n]
        g = jnp.log(lb + (1.0 - lb) * jax.nn.sigmoid(z))
        q_ref[...] = p_ref[:, 0:n] * dk ** -0.5
        k_ref[...] = (1.0 - lb) * jax.nn.sigmoid(-z)
        v_off = 2 * n
    else:
        v_off = 2 * n
        glr = p_ref[:, v_off + 2 * MIX_W:v_off + 2 * MIX_W + LANE]
        logit = jnp.dot(glr, aux1_ref[...], preferred_element_type=F32, precision=HIGHEST) + aux2_ref[...]
        g = jax.nn.log_sigmoid(logit) / GLA_NORMALIZER
        q_ref[...] = p_ref[:, 0:n] * dk ** -0.5
        k_ref[...] = p_ref[:, n:2 * n]
    hi, mid, lo = _split3(g)
    gs_ref[0] = hi
    gs_ref[1] = mid
    gs_ref[2] = lo

    _gla_chunks(n_heads, dk, dv, q_ref, k_ref, p_ref.at[:, v_off:v_off + MIX_W], gs_ref, rs_ref, o_ref, st_ref,
                tile // CHUNK)

    g_off = v_off + MIX_W
    for h in range(n_heads):
        cols = slice(h * dv, (h + 1) * dv)
        y = _head_norm_gate(o_ref[:, cols], hw_ref[...], p_ref[:, g_off + h * dv:g_off + (h + 1) * dv])
        y_ref[:, cols] = y.astype(y_ref.dtype)


def _gla_call(mode, h, norm_w, w, aux1, aux2, head_w, rs):
    bsz, seq, d = h.shape
    n_heads, dk, dv = (HG_HEADS, HG_DH, HG_DH) if mode == "hgrn" else (GLA_HEADS, GLA_DK, GLA_DV)
    n = n_heads * dk
    tile = min(MIX_TILE, seq)
    ncols = w.shape[1]
    full = lambda a: pl.BlockSpec(a.shape, lambda b, t: (0,) * a.ndim)
    return pl.pallas_call(
        functools.partial(_gla_kernel, mode),
        out_shape=jax.ShapeDtypeStruct((bsz, seq, MIX_W), BF16),
        grid=(bsz, seq // tile),
        in_specs=[pl.BlockSpec((None, tile, d), lambda b, t: (b, t, 0)),
                  full(norm_w), full(w), full(aux1), full(aux2), full(head_w), full(rs)],
        out_specs=pl.BlockSpec((None, tile, MIX_W), lambda b, t: (b, t, 0)),
        scratch_shapes=[pltpu.VMEM((tile, ncols), F32), pltpu.VMEM((tile, n), F32), pltpu.VMEM((tile, n), F32),
                        pltpu.VMEM((3, tile, n), BF16), pltpu.VMEM((tile, MIX_W), F32),
                        pltpu.VMEM((n_heads, dv, dk), F32)],
        compiler_params=_cparams(2),
        name="mix_" + mode,
    )(h, norm_w, w, aux1, aux2, head_w, rs)


def _delta_kernel(h_ref, nw_ref, w_ref, cw_ref, alog_ref, dtb_ref, hw_ref, tril_ref, y_ref,
                  p_ref, xs_ref, q_ref, k_ref, v_ref, sm_ref, o_ref, st_ref):
    tile = h_ref.shape[0]
    nh, dh = DN_HEADS, DN_DH
    qkv = 3 * MIX_W

    @pl.when(pl.program_id(1) == 0)
    def _():
        st_ref[...] = jnp.zeros_like(st_ref)
        xs_ref[0:SUBLANE, :] = jnp.zeros((SUBLANE, qkv), F32)

    xn = _rms(h_ref[...], nw_ref[...]).astype(BF16)
    p_ref[...] = _dot(xn, w_ref[...])

    xs_ref[SUBLANE:SUBLANE + tile, :] = p_ref[:, 0:qkv]
    conv = cw_ref[CONV_K - 1:CONV_K, :] * xs_ref[SUBLANE:SUBLANE + tile, :]
    for j in range(1, CONV_K):
        conv = conv + cw_ref[CONV_K - 1 - j:CONV_K - j, :] * xs_ref[SUBLANE - j:SUBLANE - j + tile, :]
    xs_ref[0:SUBLANE, :] = xs_ref[tile:tile + SUBLANE, :]
    conv = jax.nn.silu(conv)
    for h in range(nh):
        cols = slice(h * dh, (h + 1) * dh)
        qh = conv[:, h * dh:(h + 1) * dh]
        kh = conv[:, MIX_W + h * dh:MIX_W + (h + 1) * dh]
        q_ref[:, cols] = qh * lax.rsqrt(jnp.sum(qh * qh, axis=-1, keepdims=True) + NORM_EPS) * dh ** -0.5
        k_ref[:, cols] = kh * lax.rsqrt(jnp.sum(kh * kh, axis=-1, keepdims=True) + NORM_EPS)
    v_ref[...] = conv[:, 2 * MIX_W:3 * MIX_W]

    small = p_ref[:, qkv + MIX_W:qkv + MIX_W + LANE]
    gdec = -jnp.exp(alog_ref[...]) * jax.nn.softplus(small + dtb_ref[...])
    hi, mid, lo = _split3(gdec)
    sm_ref[0] = hi
    sm_ref[1] = mid
    sm_ref[2] = lo

    ii = lax.broadcasted_iota(jnp.int32, (CHUNK, CHUNK), 0)
    jj = lax.broadcasted_iota(jnp.int32, (CHUNK, CHUNK), 1)
    incl = ii >= jj
    strict = ii > jj
    eye = (ii == jj).astype(F32)
    n_chunks = tile // CHUNK

    prob = []
    for c in range(n_chunks):
        rows = slice(c * CHUNK, (c + 1) * CHUNK)
        g3 = jnp.concatenate([sm_ref[0, rows, :], sm_ref[1, rows, :], sm_ref[2, rows, :]], axis=0)
        gcum = _dot(tril_ref[...], g3)
        gcum_t = gcum.T
        beta_c = jax.nn.sigmoid(p_ref[rows, qkv + MIX_W:qkv + MIX_W + LANE])
        qc, kc, vc = q_ref[rows, :], k_ref[rows, :], v_ref[rows, :]
        for h in range(nh):
            cols = slice(h * dh, (h + 1) * dh)
            qh, kh, vh = qc[:, cols], kc[:, cols], vc[:, cols]
            gcol = gcum[:, nh + h:nh + h + 1]
            grow = gcum_t[nh + h:nh + h + 1, :]
            glast = gcum[CHUNK - 1:CHUNK, nh + h:nh + h + 1]
            bh = beta_c[:, h:h + 1]
            decay = jnp.exp(jnp.where(incl, gcol - grow, -jnp.inf))
            kb = kh * bh
            kbb, khb = kb.astype(BF16), kh.astype(BF16)
            lower = jnp.where(strict, _dot_nt(kbb, khb) * decay, 0.0)
            egc = jnp.exp(gcol)
            prob.append(dict(
                lower=lower, vb=(vh * bh).astype(BF16), kbg=(kb * egc).astype(BF16),
                qk=(_dot_nt(qh.astype(BF16), khb) * decay).astype(BF16), q_dec=(qh * egc).astype(BF16),
                k_dec=(kh * jnp.exp(glast - gcol)).astype(BF16), a=jnp.exp(glast)))

    t_inv = [eye - pr["lower"] for pr in prob]
    pw = [_mm3(_lhs3(*_split2(pr["lower"])), _rhs3(*_split2(pr["lower"]))) for pr in prob]
    for it in range(5):
        pw_s = [_split2(m) for m in pw]
        pw_rhs = [_rhs3(*s) for s in pw_s]
        t_inv = [t + _mm3(_lhs3(*_split2(t)), r) for t, r in zip(t_inv, pw_rhs)]
        if it < 4:
            pw = [_mm3(_lhs3(*s), r) for s, r in zip(pw_s, pw_rhs)]
    for pr, t in zip(prob, t_inv):
        tb = t.astype(BF16)
        pr["u"] = _dot(tb, pr["vb"])
        pr["w"] = _dot(tb, pr["kbg"]).astype(BF16)

    st = [st_ref[h] for h in range(nh)]
    for c in range(n_chunks):
        rows = slice(c * CHUNK, (c + 1) * CHUNK)
        for h in range(nh):
            pr = prob[c * nh + h]
            sb = st[h].astype(BF16)
            vnb = (pr["u"] - _dot(pr["w"], sb)).astype(BF16)
            o_ref[rows, h * dh:(h + 1) * dh] = _dot(pr["q_dec"], sb) + _dot(pr["qk"], vnb)
            st[h] = st[h] * pr["a"] + _dot_tn(pr["k_dec"], vnb)
    for h in range(nh):
        st_ref[h] = st[h]

    g_off = qkv
    for h in range(nh):
        cols = slice(h * dh, (h + 1) * dh)
        y = _head_norm_gate(o_ref[:, cols], hw_ref[...], p_ref[:, g_off + h * dh:g_off + (h + 1) * dh])
        y_ref[:, cols] = y.astype(y_ref.dtype)


def _delta_call(h, norm_w, w, conv_w, alog_row, dtb_row, head_w, tril):
    bsz, seq, d = h.shape
    tile = min(MIX_TILE, seq)
    ncols = w.shape[1]
    full = lambda a: pl.BlockSpec(a.shape, lambda b, t: (0,) * a.ndim)
    return pl.pallas_call(
        _delta_kernel,
        out_shape=jax.ShapeDtypeStruct((bsz, seq, MIX_W), BF16),
        grid=(bsz, seq // tile),
        in_specs=[pl.BlockSpec((None, tile, d), lambda b, t: (b, t, 0)),
                  full(norm_w), full(w), full(conv_w), full(alog_row), full(dtb_row), full(head_w), full(tril)],
        out_specs=pl.BlockSpec((None, tile, MIX_W), lambda b, t: (b, t, 0)),
        scratch_shapes=[pltpu.VMEM((tile, ncols), F32), pltpu.VMEM((tile + SUBLANE, 3 * MIX_W), F32),
                        pltpu.VMEM((tile, MIX_W), F32), pltpu.VMEM((tile, MIX_W), F32),
                        pltpu.VMEM((tile, MIX_W), F32), pltpu.VMEM((3, tile, LANE), BF16),
                        pltpu.VMEM((tile, MIX_W), F32), pltpu.VMEM((DN_HEADS, DN_DH, DN_DH), F32)],
        compiler_params=_cparams(2),
        name="mix_delta",
    )(h, norm_w, w, conv_w, alog_row, dtb_row, head_w, tril)


def _s5_proj_kernel(h_ref, nw_ref, w_ref, o_ref):
    xn = _rms(h_ref[...], nw_ref[...]).astype(BF16)
    o_ref[...] = _dot(xn, w_ref[...])


def _s5_proj_call(h, norm_w, w):
    bsz, seq, d = h.shape
    tile = min(ROW_TILE, seq)
    return pl.pallas_call(
        _s5_proj_kernel,
        out_shape=jax.ShapeDtypeStruct((seq, bsz * MIX_W), F32),
        grid=(bsz, seq // tile),
        in_specs=[pl.BlockSpec((None, tile, d), lambda b, t: (b, t, 0)),
                  pl.BlockSpec(norm_w.shape, lambda b, t: (0, 0)),
                  pl.BlockSpec(w.shape, lambda b, t: (0, 0))],
        out_specs=pl.BlockSpec((tile, MIX_W), lambda b, t: (t, b)),
        compiler_params=_cparams(2),
        name="s5_proj",
    )(h, norm_w, w)


def _s5_kernel(u_ref, bre_ref, bim_ref, ar_ref, ai_ref, cre_ref, cim_ref, d_ref, gw_ref, gb_ref, y_ref,
               xr_ref, xi_ref, sr_ref, si_ref):
    rows = u_ref.shape[0]
    bsz = sr_ref.shape[0]
    steps = rows // bsz
    n_cl = bre_ref.shape[0]
    cw = bre_ref.shape[1]
    sw = bre_ref.shape[2]

    @pl.when(pl.program_id(0) == 0)
    def _():
        sr_ref[...] = jnp.zeros_like(sr_ref)
        si_ref[...] = jnp.zeros_like(si_ref)

    u = u_ref[...]
    ub = u.astype(BF16)
    for c in range(n_cl):
        uc = ub[:, c * cw:(c + 1) * cw]
        xr_ref[:, c * sw:(c + 1) * sw] = _dot(uc, bre_ref[c])
        xi_ref[:, c * sw:(c + 1) * sw] = _dot(uc, bim_ref[c])

    for c in range(n_cl):
        ls = slice(c * sw, (c + 1) * sw)
        ar, ai = ar_ref[:, ls], ai_ref[:, ls]

        def step(i, carry, ls=ls, ar=ar, ai=ai):
            xr, xi = carry
            rr = pl.ds(pl.multiple_of(i * bsz, bsz), bsz)
            nxr = ar * xr - ai * xi + xr_ref[rr, ls]
            nxi = ar * xi + ai * xr + xi_ref[rr, ls]
            xr_ref[rr, ls] = nxr
            xi_ref[rr, ls] = nxi
            return nxr, nxi

        xr, xi = lax.fori_loop(0, steps, step, (sr_ref[:, ls], si_ref[:, ls]), unroll=8)
        sr_ref[:, ls] = xr
        si_ref[:, ls] = xi

    ys = []
    for c in range(n_cl):
        ls = slice(c * sw, (c + 1) * sw)
        ys.append(_dot(xr_ref[:, ls].astype(BF16), cre_ref[c]) - _dot(xi_ref[:, ls].astype(BF16), cim_ref[c]))
    y = jnp.concatenate(ys, axis=1) + d_ref[...] * u
    z = jax.nn.gelu(y)
    gl = _dot(z.astype(BF16), gw_ref[...]) + gb_ref[...]
    y_ref[...] = (z * jax.nn.sigmoid(gl)).astype(y_ref.dtype)


def _s5_call(u_tm, bsz, bre, bim, ar, ai, cre, cim, d_row, glu_w, glu_b):
    n_rows = u_tm.shape[0]
    seq = n_rows // bsz
    steps = min(S5_STEPS, seq)
    rows = steps * bsz
    n_state = ar.shape[1]
    full = lambda a: pl.BlockSpec(a.shape, lambda t: (0,) * a.ndim)
    return pl.pallas_call(
        _s5_kernel,
        out_shape=jax.ShapeDtypeStruct((n_rows, MIX_W), BF16),
        grid=(seq // steps,),
        in_specs=[pl.BlockSpec((rows, MIX_W), lambda t: (t, 0)),
                  full(bre), full(bim), full(ar), full(ai), full(cre), full(cim), full(d_row), full(glu_w),
                  full(glu_b)],
        out_specs=pl.BlockSpec((rows, MIX_W), lambda t: (t, 0)),
        scratch_shapes=[pltpu.VMEM((rows, n_state), F32), pltpu.VMEM((rows, n_state), F32),
                        pltpu.VMEM((bsz, n_state), F32), pltpu.VMEM((bsz, n_state), F32)],
        compiler_params=_cparams(1),
        name="mix_s5",
    )(u_tm, bre, bim, ar, ai, cre, cim, d_row, glu_w, glu_b)


def _s5_params(a_re, a_im, log_dt, b_re, b_im, c_re, c_im, bsz):
    dt = jnp.exp(log_dt)[:, None]
    mag = jnp.exp(a_re * dt)
    abar_r, abar_i = mag * jnp.cos(a_im * dt), mag * jnp.sin(a_im * dt)
    den = a_re * a_re + a_im * a_im
    nr, ni = abar_r - 1.0, abar_i
    zr, zi = (nr * a_re + ni * a_im) / den, (ni * a_re - nr * a_im) / den
    bbar_r = zr[..., None] * b_re - zi[..., None] * b_im
    bbar_i = zr[..., None] * b_im + zi[..., None] * b_re
    n_cl = S5_GROUPS // S5_CLUSTER
    eye = jnp.eye(S5_CLUSTER, dtype=F32)

    def pack_b(bb):
        bb = bb.reshape(n_cl, S5_CLUSTER, S5_STATE, S5_GROUP)
        out = jnp.einsum("cgph,gk->cghkp", bb, eye)
        return out.reshape(n_cl, S5_CLUSTER * S5_GROUP, S5_CLUSTER * S5_STATE).astype(BF16)

    def pack_c(cc):
        cc = cc.reshape(n_cl, S5_CLUSTER, S5_GROUP, S5_STATE)
        out = jnp.einsum("cghp,gk->cgpkh", cc, eye)
        return out.reshape(n_cl, S5_CLUSTER * S5_STATE, S5_CLUSTER * S5_GROUP).astype(BF16)

    bc = lambda a: jnp.broadcast_to(a.reshape(1, -1), (bsz, S5_GROUPS * S5_STATE))
    return pack_b(bbar_r), pack_b(bbar_i), bc(abar_r), bc(abar_i), pack_c(c_re), pack_c(c_im)


def _merge_kernel(h_ref, nw_ref, wm_ref, ya_ref, yb_ref, yc_ref, yd_ref, wb_ref, wo_ref, o_ref):
    h = h_ref[...]
    xn = _rms(h, nw_ref[...]).astype(BF16)
    d = h.shape[1]
    mixed = jnp.zeros(h.shape, F32)
    for n, y_ref in enumerate((ya_ref, yb_ref, yc_ref, yd_ref)):
        gate = jax.nn.sigmoid(_dot(xn, wm_ref[:, n * d:(n + 1) * d]))
        mixed = mixed + gate * _dot(y_ref[...], wb_ref[n])
    o_ref[...] = h + _dot(mixed.astype(BF16), wo_ref[...])


def _merge_call(h, norm_w, w_m, y_a, y_b_tm, y_c, y_d, w_b, w_o):
    bsz, seq, d = h.shape
    tile = min(ROW_TILE, seq)
    full = lambda a: pl.BlockSpec(a.shape, lambda b, t: (0,) * a.ndim)
    ytile = pl.BlockSpec((None, tile, MIX_W), lambda b, t: (b, t, 0))
    return pl.pallas_call(
        _merge_kernel,
        out_shape=jax.ShapeDtypeStruct(h.shape, F32),
        grid=(bsz, seq // tile),
        in_specs=[pl.BlockSpec((None, tile, d), lambda b, t: (b, t, 0)), full(norm_w), full(w_m),
                  ytile, pl.BlockSpec((tile, MIX_W), lambda b, t: (t, b)), ytile, ytile, full(w_b), full(w_o)],
        out_specs=pl.BlockSpec((None, tile, d), lambda b, t: (b, t, 0)),
        compiler_params=_cparams(2),
        name="merge",
    )(h, norm_w, w_m, y_a, y_b_tm, y_c, y_d, w_b, w_o)


def _ffn_kernel(h_ref, nw_ref, wg_ref, wu_ref, wd_ref, o_ref, hn_ref, acc_ref):
    j = pl.program_id(1)

    @pl.when(j == 0)
    def _():
        hn_ref[...] = _rms(h_ref[...], nw_ref[...]).astype(BF16)
        acc_ref[...] = jnp.zeros_like(acc_ref)

    hn = hn_ref[...]
    act = (jax.nn.silu(_dot(hn, wg_ref[...])) * _dot(hn, wu_ref[...])).astype(BF16)
    acc_ref[...] += _dot(act, wd_ref[...])

    @pl.when(j == pl.num_programs(1) - 1)
    def _():
        o_ref[...] = h_ref[...] + acc_ref[...]


def _ffn_call(h2, norm_w, w_gate, w_up, w_down, f_tiles):
    n_tok, d = h2.shape
    f = w_gate.shape[1]
    tf = f // f_tiles
    tile = min(ROW_TILE, n_tok)
    return pl.pallas_call(
        _ffn_kernel,
        out_shape=jax.ShapeDtypeStruct(h2.shape, F32),
        grid=(n_tok // tile, f_tiles),
        in_specs=[pl.BlockSpec((tile, d), lambda i, j: (i, 0)),
                  pl.BlockSpec(norm_w.shape, lambda i, j: (0, 0)),
                  pl.BlockSpec((d, tf), lambda i, j: (0, j)),
                  pl.BlockSpec((d, tf), lambda i, j: (0, j)),
                  pl.BlockSpec((tf, d), lambda i, j: (j, 0))],
        out_specs=pl.BlockSpec((tile, d), lambda i, j: (i, 0)),
        scratch_shapes=[pltpu.VMEM((tile, d), BF16), pltpu.VMEM((tile, d), F32)],
        compiler_params=_cparams(2),
        name="ffn_dense",
    )(h2, norm_w, w_gate, w_up, w_down)


def _route_kernel(h_ref, nw_ref, rt_ref, hn_ref, idx_ref, gate_ref):
    hn = _rms(h_ref[...], nw_ref[...])
    hn_ref[...] = hn.astype(BF16)
    logits = jnp.dot(hn, rt_ref[...], preferred_element_type=F32, precision=HIGHEST)
    lg = logits.T[0:N_EXPERTS, :]
    row = lax.broadcasted_iota(jnp.int32, lg.shape, 0)
    m1 = jnp.max(lg, axis=0, keepdims=True)
    i1 = jnp.min(jnp.where(lg == m1, row, N_EXPERTS), axis=0, keepdims=True)
    lg2 = jnp.where(row == i1, -jnp.inf, lg)
    m2 = jnp.max(lg2, axis=0, keepdims=True)
    i2 = jnp.min(jnp.where(lg2 == m2, row, N_EXPERTS), axis=0, keepdims=True)
    e2 = jnp.exp(m2 - m1)
    g1 = 1.0 / (1.0 + e2)
    g2 = e2 / (1.0 + e2)
    idx_ref[...] = jnp.where(row == 0, i1, jnp.where(row == 1, i2, 0))
    gate_ref[...] = jnp.where(row == 0, g1, jnp.where(row == 1, g2, 0.0))


def _route_call(h2, norm_w, router_pad):
    n_tok, d = h2.shape
    tile = min(ROW_TILE, n_tok)
    return pl.pallas_call(
        _route_kernel,
        out_shape=(jax.ShapeDtypeStruct((n_tok, d), BF16),
                   jax.ShapeDtypeStruct((N_EXPERTS, n_tok), jnp.int32),
                   jax.ShapeDtypeStruct((N_EXPERTS, n_tok), F32)),
        grid=(n_tok // tile,),
        in_specs=[pl.BlockSpec((tile, d), lambda i: (i, 0)),
                  pl.BlockSpec(norm_w.shape, lambda i: (0, 0)),
                  pl.BlockSpec(router_pad.shape, lambda i: (0, 0))],
        out_specs=(pl.BlockSpec((tile, d), lambda i: (i, 0)),
                   pl.BlockSpec((N_EXPERTS, tile), lambda i: (0, i)),
                   pl.BlockSpec((N_EXPERTS, tile), lambda i: (0, i))),
        compiler_params=_cparams(1),
        name="moe_route",
    )(h2, norm_w, router_pad)


def _expert_kernel(be_ref, bv_ref, x_ref, wg_ref, wu_ref, wd_ref, o_ref, acc_ref):
    i, j = pl.program_id(0), pl.program_id(1)
    last = pl.num_programs(1) - 1

    @pl.when(j == 0)
    def _():
        acc_ref[...] = jnp.zeros_like(acc_ref)

    @pl.when(bv_ref[i] != 0)
    def _():
        x = x_ref[...]
        act = (jax.nn.silu(_dot(x, wg_ref[...])) * _dot(x, wu_ref[...])).astype(BF16)
        acc_ref[...] += _dot(act, wd_ref[...])

    @pl.when(j == last)
    def _():
        o_ref[...] = acc_ref[...]


def _expert_call(blk_e, blk_valid, xs, w_gate, w_up, w_down, f_tiles):
    n_rows, d = xs.shape
    f = w_gate.shape[2]
    tf = f // f_tiles
    n_blk = n_rows // MOE_TILE
    grid_spec = pltpu.PrefetchScalarGridSpec(
        num_scalar_prefetch=2,
        grid=(n_blk, f_tiles),
        in_specs=[pl.BlockSpec((MOE_TILE, d), lambda i, j, be, bv: (i, 0)),
                  pl.BlockSpec((None, d, tf), lambda i, j, be, bv: (be[i], 0, j)),
                  pl.BlockSpec((None, d, tf), lambda i, j, be, bv: (be[i], 0, j)),
                  pl.BlockSpec((None, tf, d), lambda i, j, be, bv: (be[i], j, 0))],
        out_specs=pl.BlockSpec((MOE_TILE, d), lambda i, j, be, bv: (i, 0)),
        scratch_shapes=[pltpu.VMEM((MOE_TILE, d), F32)],
    )
    return pl.pallas_call(
        _expert_kernel,
        out_shape=jax.ShapeDtypeStruct((n_rows, d), F32),
        grid_spec=grid_spec,
        compiler_params=_cparams(2),
        name="moe_experts",
    )(blk_e, blk_valid, xs, w_gate, w_up, w_down)


def _combine_kernel(apply_norm, h_ref, y0_ref, y1_ref, g0_ref, g1_ref, nw_ref, o_ref):
    out = h_ref[...] + (g0_ref[...] * y0_ref[...] + g1_ref[...] * y1_ref[...])
    if apply_norm:
        out = _rms(out, nw_ref[...])
    o_ref[...] = out


def _combine_call(h2, y0, y1, g0, g1, norm_w, apply_norm):
    n_tok, d = h2.shape
    tile = min(ROW_TILE, n_tok)
    row = pl.BlockSpec((tile, d), lambda i: (i, 0))
    col = pl.BlockSpec((tile, 1), lambda i: (i, 0))
    return pl.pallas_call(
        functools.partial(_combine_kernel, apply_norm),
        out_shape=jax.ShapeDtypeStruct(h2.shape, F32),
        grid=(n_tok // tile,),
        in_specs=[row, row, row, col, col, pl.BlockSpec(norm_w.shape, lambda i: (0, 0))],
        out_specs=row,
        compiler_params=_cparams(1),
        name="moe_combine",
    )(h2, y0, y1, g0, g1, norm_w)


def _final_norm_kernel(h_ref, nw_ref, o_ref):
    o_ref[...] = _rms(h_ref[...], nw_ref[...])


def _final_norm_call(h2, norm_w):
    n_tok, d = h2.shape
    tile = min(ROW_TILE, n_tok)
    row = pl.BlockSpec((tile, d), lambda i: (i, 0))
    return pl.pallas_call(
        _final_norm_kernel,
        out_shape=jax.ShapeDtypeStruct(h2.shape, F32),
        grid=(n_tok // tile,),
        in_specs=[row, pl.BlockSpec(norm_w.shape, lambda i: (0, 0))],
        out_specs=row,
        compiler_params=_cparams(1),
        name="final_norm",
    )(h2, norm_w)


def _moe(h2, norm_w, router, w_gate, w_up, w_down, final_w, apply_final):
    n_tok, d = h2.shape
    router_pad = jnp.zeros((d, LANE), F32).at[:, :N_EXPERTS].set(router)
    hn, idx, gate = _route_call(h2, norm_w, router_pad)
    e_flat = idx[:TOP_K].reshape(-1)
    tok_flat = jnp.tile(jnp.arange(n_tok, dtype=jnp.int32), TOP_K)
    onehot = (e_flat[:, None] == jnp.arange(N_EXPERTS, dtype=jnp.int32)[None, :]).astype(jnp.int32)
    csum = jnp.cumsum(onehot, axis=0)
    rank = jnp.sum(csum * onehot, axis=1) - 1
    counts = csum[-1]
    padded = (counts + MOE_TILE - 1) // MOE_TILE * MOE_TILE
    pad_end = jnp.cumsum(padded)
    pad_start = pad_end - padded
    dest = pad_start[e_flat] + rank
    n_assign = n_tok * TOP_K
    n_blk = -(-n_assign // MOE_TILE) + N_EXPERTS
    src_tok = jnp.zeros((n_blk * MOE_TILE,), jnp.int32).at[dest].set(
        tok_flat, mode="promise_in_bounds", unique_indices=True)
    blk_start = jnp.arange(n_blk, dtype=jnp.int32) * MOE_TILE
    blk_e = jnp.minimum(jnp.searchsorted(pad_end, blk_start, side="right"), N_EXPERTS - 1).astype(jnp.int32)
    blk_valid = (blk_start < pad_end[-1]).astype(jnp.int32)
    xs = hn.at[src_tok].get(mode="promise_in_bounds")
    f_tiles = 2 if w_gate.shape[2] % (2 * LANE) == 0 else 1
    ys = _expert_call(blk_e, blk_valid, xs, w_gate, w_up, w_down, f_tiles)
    y0 = ys.at[dest[:n_tok]].get(mode="promise_in_bounds")
    y1 = ys.at[dest[n_tok:]].get(mode="promise_in_bounds")
    g0 = gate[0].reshape(n_tok, 1)
    g1 = gate[1].reshape(n_tok, 1)
    return _combine_call(h2, y0, y1, g0, g1, final_w, apply_final)


def _pad_cols(w, n):
    return jnp.concatenate([w, jnp.zeros((w.shape[0], n - w.shape[1]), w.dtype)], axis=1)


def _token_mix(h, norm_w, w_in, lb, hgrn_norm, s5, s5_d, s5_glu_w, s5_glu_b, dn_conv_w, dn_a_log, dn_dt_bias,
               dn_norm, gla_w_lr, gla_b_lr, gla_norm, w_branch, w_out, rs, tril):
    bsz, seq, d = h.shape
    m = MIX_W
    o = 0
    w_a = w_in[:, o:o + 4 * m]; o += 4 * m
    w_s5 = w_in[:, o:o + m]; o += m
    w_qkv = w_in[:, o:o + 3 * m]; o += 3 * m
    w_bd = w_in[:, o:o + 2 * DN_HEADS]; o += 2 * DN_HEADS
    w_dg = w_in[:, o:o + m]; o += m
    nqk = GLA_HEADS * GLA_DK
    w_gqk = w_in[:, o:o + 2 * nqk]; o += 2 * nqk
    w_gv = w_in[:, o:o + m]; o += m
    w_glr = w_in[:, o:o + GLA_RANK]; o += GLA_RANK
    w_gg = w_in[:, o:o + m]; o += m
    w_m = w_in[:, o:o + N_BRANCH * d]

    w_a = w_a.astype(BF16)
    w_c = jnp.concatenate([w_qkv, w_dg, _pad_cols(w_bd, LANE)], axis=1).astype(BF16)
    w_d = jnp.concatenate([w_gqk, w_gv, w_gg, _pad_cols(w_glr, LANE)], axis=1).astype(BF16)
    row = lambda a: a.reshape(1, -1).astype(F32)
    nw = row(norm_w)

    dummy = jnp.zeros((1, LANE), F32)
    y_a = _gla_call("hgrn", h, nw, w_a, row(lb), dummy, row(hgrn_norm), rs)

    wlr_pad = jnp.zeros((LANE, nqk), F32).at[:GLA_RANK].set(gla_w_lr)
    y_d = _gla_call("gla", h, nw, w_d, wlr_pad, row(gla_b_lr), row(gla_norm), rs)

    lane_row = lambda a: jnp.zeros((1, LANE), F32).at[0, DN_HEADS:2 * DN_HEADS].set(a)
    y_c = _delta_call(h, nw, w_c, dn_conv_w.astype(F32), lane_row(dn_a_log), lane_row(dn_dt_bias), row(dn_norm), tril)

    u_tm = _s5_proj_call(h, nw, w_s5.astype(BF16)).reshape(seq * bsz, m)
    bre, bim, ar, ai, cre, cim = s5
    y_b = _s5_call(u_tm, bsz, bre, bim, ar, ai, cre, cim, row(s5_d), s5_glu_w.astype(BF16), row(s5_glu_b))
    y_b = y_b.reshape(seq, bsz * m)

    return _merge_call(h, nw, w_m.astype(BF16), y_a, y_b, y_c, y_d, w_branch.astype(BF16), w_out.astype(BF16))


def kernel(x, attn_norm, ffn_norm, final_norm, w_in, hgrn_lb, hgrn_norm, s5_a_re, s5_a_im, s5_log_dt, s5_b_re, s5_b_im, s5_c_re, s5_c_im, s5_d, s5_glu_w, s5_glu_b, dn_conv_w, dn_a_log, dn_dt_bias, dn_norm, gla_w_lr, gla_b_lr, gla_norm, w_branch, w_out, ffn_w_gate, ffn_w_up, ffn_w_down, moe_router, moe_w_gate, moe_w_up, moe_w_down):
    bsz, seq, d = x.shape
    depth = w_in.shape[0]
    assert bsz == SUBLANE, "the S5 kernel keeps one batch row per sublane"
    assert seq % CHUNK == 0 and d == D_MODEL
    lb_soft = jax.nn.softmax(hgrn_lb.astype(F32), axis=0)
    lb_all = jnp.cumsum(lb_soft, axis=0) - lb_soft[0]
    rs = jnp.asarray(_range_sum_matrix(), BF16)
    tril = jnp.asarray(np.tile(np.tril(np.ones((CHUNK, CHUNK), np.float32)), (1, 3)), BF16)
    row = lambda a: a.reshape(1, -1).astype(F32)

    h = x
    for layer in range(depth):
        s5 = _s5_params(s5_a_re[layer], s5_a_im[layer], s5_log_dt[layer], s5_b_re[layer], s5_b_im[layer],
                        s5_c_re[layer], s5_c_im[layer], bsz)
        h = _token_mix(h, attn_norm[layer], w_in[layer], lb_all[layer], hgrn_norm[layer], s5, s5_d[layer],
                       s5_glu_w[layer], s5_glu_b[layer], dn_conv_w[layer], dn_a_log[layer], dn_dt_bias[layer],
                       dn_norm[layer], gla_w_lr[layer], gla_b_lr[layer], gla_norm[layer], w_branch[layer],
                       w_out[layer], rs, tril)
        h2 = h.reshape(bsz * seq, d)
        j = layer // 2
        last = layer == depth - 1
        if layer % 2 == 0:
            f = ffn_w_gate.shape[2]
            f_tiles = 2 if f % (2 * LANE) == 0 else 1
            h2 = _ffn_call(h2, row(ffn_norm[layer]), ffn_w_gate[j].astype(BF16), ffn_w_up[j].astype(BF16),
                           ffn_w_down[j].astype(BF16), f_tiles)
            if last:
                h2 = _final_norm_call(h2, row(final_norm))
        else:
            h2 = _moe(h2, row(ffn_norm[layer]), moe_router[j], moe_w_gate[j].astype(BF16),
                      moe_w_up[j].astype(BF16), moe_w_down[j].astype(BF16), row(final_norm), last)
        h = h2.reshape(bsz, seq, d)
    return h
```

```python
import functools
import math

import numpy as np
import jax
import jax.numpy as jnp
from jax import lax
from jax.experimental import pallas as pl
from jax.experimental.pallas import tpu as pltpu

F32 = jnp.float32
BF16 = jnp.bfloat16
HIGHEST = lax.Precision.HIGHEST

D_MODEL = 1024
MIX_W = 512
HG_HEADS = 4
HG_DH = MIX_W // HG_HEADS
S5_GROUP = 16
S5_GROUPS = MIX_W // S5_GROUP
S5_STATE = 64
DN_HEADS = 4
DN_DH = MIX_W // DN_HEADS
CONV_K = 4
GLA_HEADS = 4
GLA_DK = 64
GLA_DV = MIX_W // GLA_HEADS
GLA_RANK = 16
GLA_NORMALIZER = 16.0
CHUNK = 64
N_BRANCH = 4
N_EXPERTS = 8
TOP_K = 2
NORM_EPS = 1e-6

LANE = 128
SUBLANE = 8
VMEM_LIMIT = 56 * 1024 * 1024

MIX_TILE = 256
S5_STEPS = 64
S5_CLUSTER = 8
ROW_TILE = 512
MOE_TILE = 512
LEVELS = (32, 16, 8, 4, 2, 1)


def _cparams(n_axes):
    return pltpu.CompilerParams(dimension_semantics=("arbitrary",) * n_axes, vmem_limit_bytes=VMEM_LIMIT)


def _rms(x, w):
    return x * lax.rsqrt(jnp.mean(x * x, axis=-1, keepdims=True) + NORM_EPS) * w


def _dot(a, b):
    return jnp.dot(a, b, preferred_element_type=F32)


def _dot_nt(a, b, precision=None):
    return lax.dot_general(a, b, (((1,), (1,)), ((), ())), preferred_element_type=F32, precision=precision)


def _dot_tn(a, b):
    return lax.dot_general(a, b, (((0,), (0,)), ((), ())), preferred_element_type=F32)


def _split3(x):
    hi = x.astype(BF16)
    r1 = x - hi.astype(F32)
    mid = r1.astype(BF16)
    lo = (r1 - mid.astype(F32)).astype(BF16)
    return hi, mid, lo


def _split2(x):
    hi = x.astype(BF16)
    return hi, (x - hi.astype(F32)).astype(BF16)


def _lhs3(hi, lo):
    return jnp.concatenate([hi, hi, lo], axis=1)


def _rhs3(hi, lo):
    return jnp.concatenate([hi, lo, hi], axis=0)


def _mm3(lhs3, rhs3):
    return jnp.dot(lhs3, rhs3, preferred_element_type=F32)


def _range_sum_matrix():
    c = CHUNK
    m = np.zeros((2 + len(LEVELS), c, c), np.float32)
    for r in range(c):
        m[0, r, :r + 1] = 1.0
        m[1, r, r + 1:] = 1.0
        for l, b in enumerate(LEVELS):
            mid = (r // (2 * b)) * 2 * b + b - 1
            if r > mid:
                m[2 + l, r, mid + 1:r + 1] = 1.0
            else:
                m[2 + l, r, r + 1:mid + 1] = 1.0
    m = m.reshape(-1, c)
    return np.concatenate([m, m, m], axis=1)


def _level_masks():
    i = lax.broadcasted_iota(jnp.int32, (CHUNK, CHUNK), 0)
    j = lax.broadcasted_iota(jnp.int32, (CHUNK, CHUNK), 1)
    masks = []
    for b in LEVELS:
        same = (i // (2 * b)) == (j // (2 * b))
        masks.append(same & ((i % (2 * b)) >= b) & ((j % (2 * b)) < b))
    return masks


def _head_norm_gate(o, w, gate):
    return o * lax.rsqrt(jnp.mean(o * o, axis=-1, keepdims=True) + NORM_EPS) * w * jax.nn.silu(gate)


def _gla_chunks(n_heads, dk, dv, q_ref, k_ref, v_ref, gs_ref, rs_ref, o_ref, st_ref, n_chunks):
    masks = _level_masks()

    pre = []
    for c in range(n_chunks):
        rows = slice(c * CHUNK, (c + 1) * CHUNK)
        g3 = jnp.concatenate([gs_ref[0, rows, :], gs_ref[1, rows, :], gs_ref[2, rows, :]], axis=0)
        x = jnp.exp(_dot(rs_ref[...], g3))
        qc, kc, vc = q_ref[rows, :], k_ref[rows, :], v_ref[rows, :]
        heads = []
        for h in range(n_heads):
            cols = slice(h * dk, (h + 1) * dk)
            qh, kh = qc[:, cols], kc[:, cols]
            vh = vc[:, h * dv:(h + 1) * dv]
            vb = vh.astype(BF16)
            qd = (qh * x[0:CHUNK, cols]).astype(BF16)
            kd = (kh * x[CHUNK:2 * CHUNK, cols]).astype(BF16)
            a = jnp.zeros((CHUNK, CHUNK), F32)
            for l in range(len(LEVELS)):
                xl = x[(2 + l) * CHUNK:(3 + l) * CHUNK, cols]
                al = _dot_nt((qh * xl).astype(BF16), (kh * xl).astype(BF16))
                a = jnp.where(masks[l], al, a)
            diag = jnp.sum(qh * kh, axis=-1, keepdims=True)
            o_ref[rows, h * dv:(h + 1) * dv] = _dot(a.astype(BF16), vb) + diag * vh
            heads.append((qd, kd, vb, x[CHUNK - 1:CHUNK, cols]))
        pre.append(heads)

    st = [st_ref[h] for h in range(n_heads)]
    for c in range(n_chunks):
        rows = slice(c * CHUNK, (c + 1) * CHUNK)
        for h in range(n_heads):
            qd, kd, vb, dec = pre[c][h]
            o_ref[rows, h * dv:(h + 1) * dv] += _dot_nt(qd, st[h].astype(BF16))
            st[h] = st[h] * dec + _dot_tn(vb, kd)
    for h in range(n_heads):
        st_ref[h] = st[h]


def _gla_kernel(mode, h_ref, nw_ref, w_ref, aux1_ref, aux2_ref, hw_ref, rs_ref, y_ref,
                p_ref, q_ref, k_ref, gs_ref, o_ref, st_ref):
    if mode == "hgrn":
        n_heads, dk, dv = HG_HEADS, HG_DH, HG_DH
    else:
        n_heads, dk, dv = GLA_HEADS, GLA_DK, GLA_DV
    n = n_heads * dk
    tile = h_ref.shape[0]

    @pl.when(pl.program_id(1) == 0)
    def _():
        st_ref[...] = jnp.zeros_like(st_ref)

    xn = _rms(h_ref[...], nw_ref[...]).astype(BF16)
    p_ref[...] = _dot(xn, w_ref[...])
    if mode == "hgrn":
        lb = aux1_ref[...]
        z = p_ref[:, n:2 * n]
        g = jnp.log(lb + (1.0 - lb) * jax.nn.sigmoid(z))
        q_ref[...] = p_ref[:, 0:n] * dk ** -0.5
        k_ref[...] = (1.0 - lb) * jax.nn.sigmoid(-z)
        v_off = 2 * n
    else:
        v_off = 2 * n
        glr = p_ref[:, v_off + 2 * MIX_W:v_off + 2 * MIX_W + LANE]
        logit = jnp.dot(glr, aux1_ref[...], preferred_element_type=F32, precision=HIGHEST) + aux2_ref[...]
        g = jax.nn.log_sigmoid(logit) / GLA_NORMALIZER
        q_ref[...] = p_ref[:, 0:n] * dk ** -0.5
        k_ref[...] = p_ref[:, n:2 * n]
    hi, mid, lo = _split3(g)
    gs_ref[0] = hi
    gs_ref[1] = mid
    gs_ref[2] = lo

    _gla_chunks(n_heads, dk, dv, q_ref, k_ref, p_ref.at[:, v_off:v_off + MIX_W], gs_ref, rs_ref, o_ref, st_ref,
                tile // CHUNK)

    g_off = v_off + MIX_W
    for h in range(n_heads):
        cols = slice(h * dv, (h + 1) * dv)
        y = _head_norm_gate(o_ref[:, cols], hw_ref[...], p_ref[:, g_off + h * dv:g_off + (h + 1) * dv])
        y_ref[:, cols] = y.astype(y_ref.dtype)


def _gla_call(mode, h, norm_w, w, aux1, aux2, head_w, rs):
    bsz, seq, d = h.shape
    n_heads, dk, dv = (HG_HEADS, HG_DH, HG_DH) if mode == "hgrn" else (GLA_HEADS, GLA_DK, GLA_DV)
    n = n_heads * dk
    tile = min(MIX_TILE, seq)
    ncols = w.shape[1]
    full = lambda a: pl.BlockSpec(a.shape, lambda b, t: (0,) * a.ndim)
    return pl.pallas_call(
        functools.partial(_gla_kernel, mode),
        out_shape=jax.ShapeDtypeStruct((bsz, seq, MIX_W), BF16),
        grid=(bsz, seq // tile),
        in_specs=[pl.BlockSpec((None, tile, d), lambda b, t: (b, t, 0)),
                  full(norm_w), full(w), full(aux1), full(aux2), full(head_w), full(rs)],
        out_specs=pl.BlockSpec((None, tile, MIX_W), lambda b, t: (b, t, 0)),
        scratch_shapes=[pltpu.VMEM((tile, ncols), F32), pltpu.VMEM((tile, n), F32), pltpu.VMEM((tile, n), F32),
                        pltpu.VMEM((3, tile, n), BF16), pltpu.VMEM((tile, MIX_W), F32),
                        pltpu.VMEM((n_heads, dv, dk), F32)],
        compiler_params=_cparams(2),
        name="mix_" + mode,
    )(h, norm_w, w, aux1, aux2, head_w, rs)


def _delta_kernel(h_ref, nw_ref, w_ref, cw_ref, alog_ref, dtb_ref, hw_ref, tril_ref, y_ref,
                  p_ref, xs_ref, q_ref, k_ref, v_ref, sm_ref, o_ref, st_ref):
    tile = h_ref.shape[0]
    nh, dh = DN_HEADS, DN_DH
    qkv = 3 * MIX_W

    @pl.when(pl.program_id(1) == 0)
    def _():
        st_ref[...] = jnp.zeros_like(st_ref)
        xs_ref[0:SUBLANE, :] = jnp.zeros((SUBLANE, qkv), F32)

    xn = _rms(h_ref[...], nw_ref[...]).astype(BF16)
    p_ref[...] = _dot(xn, w_ref[...])

    xs_ref[SUBLANE:SUBLANE + tile, :] = p_ref[:, 0:qkv]
    conv = cw_ref[CONV_K - 1:CONV_K, :] * xs_ref[SUBLANE:SUBLANE + tile, :]
    for j in range(1, CONV_K):
        conv = conv + cw_ref[CONV_K - 1 - j:CONV_K - j, :] * xs_ref[SUBLANE - j:SUBLANE - j + tile, :]
    xs_ref[0:SUBLANE, :] = xs_ref[tile:tile + SUBLANE, :]
    conv = jax.nn.silu(conv)
    for h in range(nh):
        cols = slice(h * dh, (h + 1) * dh)
        qh = conv[:, h * dh:(h + 1) * dh]
        kh = conv[:, MIX_W + h * dh:MIX_W + (h + 1) * dh]
        q_ref[:, cols] = qh * lax.rsqrt(jnp.sum(qh * qh, axis=-1, keepdims=True) + NORM_EPS) * dh ** -0.5
        k_ref[:, cols] = kh * lax.rsqrt(jnp.sum(kh * kh, axis=-1, keepdims=True) + NORM_EPS)
    v_ref[...] = conv[:, 2 * MIX_W:3 * MIX_W]

    small = p_ref[:, qkv + MIX_W:qkv + MIX_W + LANE]
    gdec = -jnp.exp(alog_ref[...]) * jax.nn.softplus(small + dtb_ref[...])
    hi, mid, lo = _split3(gdec)
    sm_ref[0] = hi
    sm_ref[1] = mid
    sm_ref[2] = lo

    pw_ = nh * CHUNK
    ii = lax.broadcasted_iota(jnp.int32, (CHUNK, pw_), 0)
    lane_p = lax.broadcasted_iota(jnp.int32, (CHUNK, pw_), 1)
    jj = lane_p % CHUNK
    incl = ii >= jj
    strict = ii > jj
    eye = (ii == jj).astype(F32)
    head_p = (lane_p // CHUNK).astype(F32).astype(BF16)
    head_w = (lax.broadcasted_iota(jnp.int32, (CHUNK, MIX_W), 1) // dh).astype(F32).astype(BF16)
    sel_p = [head_p == h for h in range(nh)]
    sel_w = [head_w == h for h in range(nh)]
    n_chunks = tile // CHUNK

    def block_diag(y, sel):
        return jnp.concatenate([jnp.where(sel[h], y, jnp.zeros_like(y)) for h in range(nh)], axis=0)

    def head_cols(x, lo, width):
        return jnp.concatenate([jnp.broadcast_to(x[:, lo + h:lo + h + 1], (x.shape[0], width)) for h in range(nh)],
                               axis=1)

    prob = []
    for c in range(n_chunks):
        rows = slice(c * CHUNK, (c + 1) * CHUNK)
        g3 = jnp.concatenate([sm_ref[0, rows, :], sm_ref[1, rows, :], sm_ref[2, rows, :]], axis=0)
        gcum = _dot(tril_ref[...], g3)
        gcum_t = gcum.T
        beta_c = jax.nn.sigmoid(p_ref[rows, qkv + MIX_W:qkv + MIX_W + LANE])
        qc, kc, vc = q_ref[rows, :], k_ref[rows, :], v_ref[rows, :]
        gcol_p = head_cols(gcum, nh, CHUNK)
        grow_p = jnp.concatenate([gcum_t[nh + h:nh + h + 1, :] for h in range(nh)], axis=1)
        decay = jnp.exp(jnp.where(incl, gcol_p - grow_p, -jnp.inf))
        gcol_w = head_cols(gcum, nh, dh)
        glast_w = gcol_w[CHUNK - 1:CHUNK, :]
        beta_w = head_cols(beta_c, 0, dh)
        egc = jnp.exp(gcol_w)
        kb = kc * beta_w
        k_bd = block_diag(kc.astype(BF16), sel_w)
        kk_qk = _dot_nt(jnp.concatenate([kb.astype(BF16), qc.astype(BF16)], axis=0), k_bd)
        lower = jnp.where(strict, kk_qk[0:CHUNK] * decay, 0.0)
        prob.append(dict(
            lower=lower, vb_bd=block_diag((vc * beta_w).astype(BF16), sel_w),
            kbg_bd=block_diag((kb * egc).astype(BF16), sel_w),
            qk=(kk_qk[CHUNK:2 * CHUNK] * decay).astype(BF16), q_dec=(qc * egc).astype(BF16),
            k_dec=(kc * jnp.exp(glast_w - gcol_w)).astype(BF16), a=jnp.exp(glast_w)))

    def rhs3_bd(hi, lo):
        bh, bl = block_diag(hi, sel_p), block_diag(lo, sel_p)
        return jnp.concatenate([bh, bl, bh], axis=0)

    t_inv = [eye - pr["lower"] for pr in prob]
    pw = []
    for pr in prob:
        hi, lo = _split2(pr["lower"])
        pw.append(_mm3(_lhs3(hi, lo), rhs3_bd(hi, lo)))
    for it in range(5):
        for i in range(n_chunks):
            hi, lo = _split2(pw[i])
            lhs_t = _lhs3(*_split2(t_inv[i]))
            if it < 4:
                both = _mm3(jnp.concatenate([lhs_t, _lhs3(hi, lo)], axis=0), rhs3_bd(hi, lo))
                t_inv[i] = t_inv[i] + both[0:CHUNK]
                pw[i] = both[CHUNK:2 * CHUNK]
            else:
                t_inv[i] = t_inv[i] + _mm3(lhs_t, rhs3_bd(hi, lo))
    for pr, t in zip(prob, t_inv):
        tb = t.astype(BF16)
        pr["u"] = _dot(tb, pr["vb_bd"])
        pr["w"] = _dot(tb, pr["kbg_bd"]).astype(BF16)

    st = [st_ref[h] for h in range(nh)]
    for c in range(n_chunks):
        rows = slice(c * CHUNK, (c + 1) * CHUNK)
        pr = prob[c]
        for h in range(nh):
            cols = slice(h * dh, (h + 1) * dh)
            sb = st[h].astype(BF16)
            vnb = (pr["u"][:, cols] - _dot(pr["w"][:, cols], sb)).astype(BF16)
            o_ref[rows, cols] = (_dot(pr["q_dec"][:, cols], sb)
                                 + _dot(pr["qk"][:, h * CHUNK:(h + 1) * CHUNK], vnb))
            st[h] = st[h] * pr["a"][:, cols] + _dot_tn(pr["k_dec"][:, cols], vnb)
    for h in range(nh):
        st_ref[h] = st[h]

    g_off = qkv
    for h in range(nh):
        cols = slice(h * dh, (h + 1) * dh)
        y = _head_norm_gate(o_ref[:, cols], hw_ref[...], p_ref[:, g_off + h * dh:g_off + (h + 1) * dh])
        y_ref[:, cols] = y.astype(y_ref.dtype)


def _delta_call(h, norm_w, w, conv_w, alog_row, dtb_row, head_w, tril):
    bsz, seq, d = h.shape
    tile = min(MIX_TILE, seq)
    ncols = w.shape[1]
    full = lambda a: pl.BlockSpec(a.shape, lambda b, t: (0,) * a.ndim)
    return pl.pallas_call(
        _delta_kernel,
        out_shape=jax.ShapeDtypeStruct((bsz, seq, MIX_W), BF16),
        grid=(bsz, seq // tile),
        in_specs=[pl.BlockSpec((None, tile, d), lambda b, t: (b, t, 0)),
                  full(norm_w), full(w), full(conv_w), full(alog_row), full(dtb_row), full(head_w), full(tril)],
        out_specs=pl.BlockSpec((None, tile, MIX_W), lambda b, t: (b, t, 0)),
        scratch_shapes=[pltpu.VMEM((tile, ncols), F32), pltpu.VMEM((tile + SUBLANE, 3 * MIX_W), F32),
                        pltpu.VMEM((tile, MIX_W), F32), pltpu.VMEM((tile, MIX_W), F32),
                        pltpu.VMEM((tile, MIX_W), F32), pltpu.VMEM((3, tile, LANE), BF16),
                        pltpu.VMEM((tile, MIX_W), F32), pltpu.VMEM((DN_HEADS, DN_DH, DN_DH), F32)],
        compiler_params=_cparams(2),
        name="mix_delta",
    )(h, norm_w, w, conv_w, alog_row, dtb_row, head_w, tril)


def _s5_kernel(h_ref, nw_ref, w_ref, bre_ref, bim_ref, ar_ref, ai_ref, cre_ref, cim_ref, d_ref, gw_ref, gb_ref,
               y_ref, u_ref, xr_ref, xi_ref, sr_ref, si_ref):
    bsz, steps, d = h_ref.shape
    rows = bsz * steps
    n_cl = bre_ref.shape[0]
    cw = bre_ref.shape[1]
    sw = bre_ref.shape[2]

    @pl.when(pl.program_id(0) == 0)
    def _():
        sr_ref[...] = jnp.zeros_like(sr_ref)
        si_ref[...] = jnp.zeros_like(si_ref)

    xn = _rms(h_ref[...], nw_ref[...]).astype(BF16).reshape(rows, d)
    p = _dot(xn, w_ref[...])
    n_lt = u_ref.shape[0]
    for b in range(bsz):
        for j in range(n_lt):
            u_ref[j, pl.ds(b, steps, stride=bsz), :] = p[b * steps:(b + 1) * steps, j * LANE:(j + 1) * LANE]

    u = jnp.concatenate([u_ref[j] for j in range(n_lt)], axis=1)
    ub = u.astype(BF16)
    for c in range(n_cl):
        uc = ub[:, c * cw:(c + 1) * cw]
        xr_ref[:, c * sw:(c + 1) * sw] = _dot(uc, bre_ref[c])
        xi_ref[:, c * sw:(c + 1) * sw] = _dot(uc, bim_ref[c])

    for c in range(n_cl):
        ls = slice(c * sw, (c + 1) * sw)
        ar, ai = ar_ref[:, ls], ai_ref[:, ls]

        def step(i, carry, ls=ls, ar=ar, ai=ai):
            xr, xi = carry
            rr = pl.ds(pl.multiple_of(i * bsz, bsz), bsz)
            nxr = ar * xr - ai * xi + xr_ref[rr, ls]
            nxi = ar * xi + ai * xr + xi_ref[rr, ls]
            xr_ref[rr, ls] = nxr
            xi_ref[rr, ls] = nxi
            return nxr, nxi

        xr, xi = lax.fori_loop(0, steps, step, (sr_ref[:, ls], si_ref[:, ls]), unroll=8)
        sr_ref[:, ls] = xr
        si_ref[:, ls] = xi

    ys = []
    for c in range(n_cl):
        ls = slice(c * sw, (c + 1) * sw)
        ys.append(_dot(xr_ref[:, ls].astype(BF16), cre_ref[c]) - _dot(xi_ref[:, ls].astype(BF16), cim_ref[c]))
    y = jnp.concatenate(ys, axis=1) + d_ref[...] * u
    z = jax.nn.gelu(y)
    gl = _dot(z.astype(BF16), gw_ref[...]) + gb_ref[...]
    out = z * jax.nn.sigmoid(gl)
    for j in range(n_lt):
        u_ref[j] = out[:, j * LANE:(j + 1) * LANE]
    for b in range(bsz):
        for j in range(n_lt):
            y_ref[b, :, j * LANE:(j + 1) * LANE] = u_ref[j, pl.ds(b, steps, stride=bsz), :].astype(y_ref.dtype)


def _s5_call(h, norm_w, w, bre, bim, ar, ai, cre, cim, d_row, glu_w, glu_b):
    bsz, seq, d = h.shape
    steps = min(S5_STEPS, seq)
    rows = steps * bsz
    n_state = ar.shape[1]
    full = lambda a: pl.BlockSpec(a.shape, lambda t: (0,) * a.ndim)
    return pl.pallas_call(
        _s5_kernel,
        out_shape=jax.ShapeDtypeStruct((bsz, seq, MIX_W), BF16),
        grid=(seq // steps,),
        in_specs=[pl.BlockSpec((bsz, steps, d), lambda t: (0, t, 0)), full(norm_w), full(w),
                  full(bre), full(bim), full(ar), full(ai), full(cre), full(cim), full(d_row), full(glu_w),
                  full(glu_b)],
        out_specs=pl.BlockSpec((bsz, steps, MIX_W), lambda t: (0, t, 0)),
        scratch_shapes=[pltpu.VMEM((MIX_W // LANE, rows, LANE), F32),
                        pltpu.VMEM((rows, n_state), F32), pltpu.VMEM((rows, n_state), F32),
                        pltpu.VMEM((bsz, n_state), F32), pltpu.VMEM((bsz, n_state), F32)],
        compiler_params=_cparams(1),
        name="mix_s5",
    )(h, norm_w, w, bre, bim, ar, ai, cre, cim, d_row, glu_w, glu_b)


def _s5_params(a_re, a_im, log_dt, b_re, b_im, c_re, c_im, bsz):
    dt = jnp.exp(log_dt)[:, None]
    mag = jnp.exp(a_re * dt)
    abar_r, abar_i = mag * jnp.cos(a_im * dt), mag * jnp.sin(a_im * dt)
    den = a_re * a_re + a_im * a_im
    nr, ni = abar_r - 1.0, abar_i
    zr, zi = (nr * a_re + ni * a_im) / den, (ni * a_re - nr * a_im) / den
    bbar_r = zr[..., None] * b_re - zi[..., None] * b_im
    bbar_i = zr[..., None] * b_im + zi[..., None] * b_re
    n_cl = S5_GROUPS // S5_CLUSTER
    eye = jnp.eye(S5_CLUSTER, dtype=F32)

    def pack_b(bb):
        bb = bb.reshape(n_cl, S5_CLUSTER, S5_STATE, S5_GROUP)
        out = jnp.einsum("cgph,gk->cghkp", bb, eye)
        return out.reshape(n_cl, S5_CLUSTER * S5_GROUP, S5_CLUSTER * S5_STATE).astype(BF16)

    def pack_c(cc):
        cc = cc.reshape(n_cl, S5_CLUSTER, S5_GROUP, S5_STATE)
        out = jnp.einsum("cghp,gk->cgpkh", cc, eye)
        return out.reshape(n_cl, S5_CLUSTER * S5_STATE, S5_CLUSTER * S5_GROUP).astype(BF16)

    bc = lambda a: jnp.broadcast_to(a.reshape(1, -1), (bsz, S5_GROUPS * S5_STATE))
    return pack_b(bbar_r), pack_b(bbar_i), bc(abar_r), bc(abar_i), pack_c(c_re), pack_c(c_im)


def _merge_kernel(h_ref, nw_ref, wm_ref, ya_ref, yb_ref, yc_ref, yd_ref, wb_ref, wo_ref, o_ref):
    h = h_ref[...]
    xn = _rms(h, nw_ref[...]).astype(BF16)
    d = h.shape[1]
    mixed = jnp.zeros(h.shape, F32)
    for n, y_ref in enumerate((ya_ref, yb_ref, yc_ref, yd_ref)):
        gate = jax.nn.sigmoid(_dot(xn, wm_ref[:, n * d:(n + 1) * d]))
        mixed = mixed + gate * _dot(y_ref[...], wb_ref[n])
    o_ref[...] = h + _dot(mixed.astype(BF16), wo_ref[...])


def _merge_call(h, norm_w, w_m, y_a, y_b, y_c, y_d, w_b, w_o):
    bsz, seq, d = h.shape
    tile = min(ROW_TILE, seq)
    full = lambda a: pl.BlockSpec(a.shape, lambda b, t: (0,) * a.ndim)
    ytile = pl.BlockSpec((None, tile, MIX_W), lambda b, t: (b, t, 0))
    return pl.pallas_call(
        _merge_kernel,
        out_shape=jax.ShapeDtypeStruct(h.shape, F32),
        grid=(bsz, seq // tile),
        in_specs=[pl.BlockSpec((None, tile, d), lambda b, t: (b, t, 0)), full(norm_w), full(w_m),
                  ytile, ytile, ytile, ytile, full(w_b), full(w_o)],
        out_specs=pl.BlockSpec((None, tile, d), lambda b, t: (b, t, 0)),
        compiler_params=_cparams(2),
        name="merge",
    )(h, norm_w, w_m, y_a, y_b, y_c, y_d, w_b, w_o)


def _ffn_kernel(h_ref, nw_ref, wg_ref, wu_ref, wd_ref, o_ref, hn_ref, acc_ref):
    j = pl.program_id(1)

    @pl.when(j == 0)
    def _():
        hn_ref[...] = _rms(h_ref[...], nw_ref[...]).astype(BF16)
        acc_ref[...] = jnp.zeros_like(acc_ref)

    hn = hn_ref[...]
    act = (jax.nn.silu(_dot(hn, wg_ref[...])) * _dot(hn, wu_ref[...])).astype(BF16)
    acc_ref[...] += _dot(act, wd_ref[...])

    @pl.when(j == pl.num_programs(1) - 1)
    def _():
        o_ref[...] = h_ref[...] + acc_ref[...]


def _ffn_call(h2, norm_w, w_gate, w_up, w_down, f_tiles):
    n_tok, d = h2.shape
    f = w_gate.shape[1]
    tf = f // f_tiles
    tile = min(ROW_TILE, n_tok)
    return pl.pallas_call(
        _ffn_kernel,
        out_shape=jax.ShapeDtypeStruct(h2.shape, F32),
        grid=(n_tok // tile, f_tiles),
        in_specs=[pl.BlockSpec((tile, d), lambda i, j: (i, 0)),
                  pl.BlockSpec(norm_w.shape, lambda i, j: (0, 0)),
                  pl.BlockSpec((d, tf), lambda i, j: (0, j)),
                  pl.BlockSpec((d, tf), lambda i, j: (0, j)),
                  pl.BlockSpec((tf, d), lambda i, j: (j, 0))],
        out_specs=pl.BlockSpec((tile, d), lambda i, j: (i, 0)),
        scratch_shapes=[pltpu.VMEM((tile, d), BF16), pltpu.VMEM((tile, d), F32)],
        compiler_params=_cparams(2),
        name="ffn_dense",
    )(h2, norm_w, w_gate, w_up, w_down)


def _route_kernel(h_ref, nw_ref, rt_ref, hn_ref, idx_ref, gate_ref):
    hn = _rms(h_ref[...], nw_ref[...])
    hn_ref[...] = hn.astype(BF16)
    logits = jnp.dot(hn, rt_ref[...], preferred_element_type=F32, precision=HIGHEST)
    lg = logits.T[0:N_EXPERTS, :]
    row = lax.broadcasted_iota(jnp.int32, lg.shape, 0)
    m1 = jnp.max(lg, axis=0, keepdims=True)
    i1 = jnp.min(jnp.where(lg == m1, row, N_EXPERTS), axis=0, keepdims=True)
    lg2 = jnp.where(row == i1, -jnp.inf, lg)
    m2 = jnp.max(lg2, axis=0, keepdims=True)
    i2 = jnp.min(jnp.where(lg2 == m2, row, N_EXPERTS), axis=0, keepdims=True)
    e2 = jnp.exp(m2 - m1)
    g1 = 1.0 / (1.0 + e2)
    g2 = e2 / (1.0 + e2)
    idx_ref[...] = jnp.where(row == 0, i1, jnp.where(row == 1, i2, 0))
    gate_ref[...] = jnp.where(row == 0, g1, jnp.where(row == 1, g2, 0.0))


def _route_call(h2, norm_w, router_pad):
    n_tok, d = h2.shape
    tile = min(ROW_TILE, n_tok)
    return pl.pallas_call(
        _route_kernel,
        out_shape=(jax.ShapeDtypeStruct((n_tok, d), BF16),
                   jax.ShapeDtypeStruct((N_EXPERTS, n_tok), jnp.int32),
                   jax.ShapeDtypeStruct((N_EXPERTS, n_tok), F32)),
        grid=(n_tok // tile,),
        in_specs=[pl.BlockSpec((tile, d), lambda i: (i, 0)),
                  pl.BlockSpec(norm_w.shape, lambda i: (0, 0)),
                  pl.BlockSpec(router_pad.shape, lambda i: (0, 0))],
        out_specs=(pl.BlockSpec((tile, d), lambda i: (i, 0)),
                   pl.BlockSpec((N_EXPERTS, tile), lambda i: (0, i)),
                   pl.BlockSpec((N_EXPERTS, tile), lambda i: (0, i))),
        compiler_params=_cparams(1),
        name="moe_route",
    )(h2, norm_w, router_pad)


def _expert_kernel(be_ref, bv_ref, x_ref, wg_ref, wu_ref, wd_ref, o_ref, acc_ref):
    i, j = pl.program_id(0), pl.program_id(1)
    last = pl.num_programs(1) - 1

    @pl.when(j == 0)
    def _():
        acc_ref[...] = jnp.zeros_like(acc_ref)

    @pl.when(bv_ref[i] != 0)
    def _():
        x = x_ref[...]
        act = (jax.nn.silu(_dot(x, wg_ref[...])) * _dot(x, wu_ref[...])).astype(BF16)
        acc_ref[...] += _dot(act, wd_ref[...])

    @pl.when(j == last)
    def _():
        o_ref[...] = acc_ref[...].astype(o_ref.dtype)


def _expert_call(blk_e, blk_valid, xs, w_gate, w_up, w_down, f_tiles):
    n_rows, d = xs.shape
    f = w_gate.shape[2]
    tf = f // f_tiles
    n_blk = n_rows // MOE_TILE
    grid_spec = pltpu.PrefetchScalarGridSpec(
        num_scalar_prefetch=2,
        grid=(n_blk, f_tiles),
        in_specs=[pl.BlockSpec((MOE_TILE, d), lambda i, j, be, bv: (i, 0)),
                  pl.BlockSpec((None, d, tf), lambda i, j, be, bv: (be[i], 0, j)),
                  pl.BlockSpec((None, d, tf), lambda i, j, be, bv: (be[i], 0, j)),
                  pl.BlockSpec((None, tf, d), lambda i, j, be, bv: (be[i], j, 0))],
        out_specs=pl.BlockSpec((MOE_TILE, d), lambda i, j, be, bv: (i, 0)),
        scratch_shapes=[pltpu.VMEM((MOE_TILE, d), F32)],
    )
    return pl.pallas_call(
        _expert_kernel,
        out_shape=jax.ShapeDtypeStruct((n_rows, d), BF16),
        grid_spec=grid_spec,
        compiler_params=_cparams(2),
        name="moe_experts",
    )(blk_e, blk_valid, xs, w_gate, w_up, w_down)


def _combine_kernel(apply_norm, h_ref, y0_ref, y1_ref, g0_ref, g1_ref, nw_ref, o_ref):
    out = h_ref[...] + (g0_ref[...] * y0_ref[...].astype(F32) + g1_ref[...] * y1_ref[...].astype(F32))
    if apply_norm:
        out = _rms(out, nw_ref[...])
    o_ref[...] = out


def _combine_call(h2, y0, y1, g0, g1, norm_w, apply_norm):
    n_tok, d = h2.shape
    tile = min(ROW_TILE, n_tok)
    row = pl.BlockSpec((tile, d), lambda i: (i, 0))
    col = pl.BlockSpec((tile, 1), lambda i: (i, 0))
    return pl.pallas_call(
        functools.partial(_combine_kernel, apply_norm),
        out_shape=jax.ShapeDtypeStruct(h2.shape, F32),
        grid=(n_tok // tile,),
        in_specs=[row, row, row, col, col, pl.BlockSpec(norm_w.shape, lambda i: (0, 0))],
        out_specs=row,
        compiler_params=_cparams(1),
        name="moe_combine",
    )(h2, y0, y1, g0, g1, norm_w)


def _final_norm_kernel(h_ref, nw_ref, o_ref):
    o_ref[...] = _rms(h_ref[...], nw_ref[...])


def _final_norm_call(h2, norm_w):
    n_tok, d = h2.shape
    tile = min(ROW_TILE, n_tok)
    row = pl.BlockSpec((tile, d), lambda i: (i, 0))
    return pl.pallas_call(
        _final_norm_kernel,
        out_shape=jax.ShapeDtypeStruct(h2.shape, F32),
        grid=(n_tok // tile,),
        in_specs=[row, pl.BlockSpec(norm_w.shape, lambda i: (0, 0))],
        out_specs=row,
        compiler_params=_cparams(1),
        name="final_norm",
    )(h2, norm_w)


def _moe(h2, norm_w, router, w_gate, w_up, w_down, final_w, apply_final):
    n_tok, d = h2.shape
    router_pad = jnp.zeros((d, LANE), F32).at[:, :N_EXPERTS].set(router)
    hn, idx, gate = _route_call(h2, norm_w, router_pad)
    e_flat = idx[:TOP_K].reshape(-1)
    tok_flat = jnp.tile(jnp.arange(n_tok, dtype=jnp.int32), TOP_K)
    onehot = (e_flat[:, None] == jnp.arange(N_EXPERTS, dtype=jnp.int32)[None, :]).astype(jnp.int32)
    csum = jnp.cumsum(onehot, axis=0)
    rank = jnp.sum(csum * onehot, axis=1) - 1
    counts = csum[-1]
    padded = (counts + MOE_TILE - 1) // MOE_TILE * MOE_TILE
    pad_end = jnp.cumsum(padded)
    pad_start = pad_end - padded
    dest = pad_start[e_flat] + rank
    n_assign = n_tok * TOP_K
    n_blk = -(-n_assign // MOE_TILE) + N_EXPERTS
    src_tok = jnp.zeros((n_blk * MOE_TILE,), jnp.int32).at[dest].set(
        tok_flat, mode="promise_in_bounds", unique_indices=True)
    blk_start = jnp.arange(n_blk, dtype=jnp.int32) * MOE_TILE
    blk_e = jnp.minimum(jnp.searchsorted(pad_end, blk_start, side="right"), N_EXPERTS - 1).astype(jnp.int32)
    blk_valid = (blk_start < pad_end[-1]).astype(jnp.int32)
    xs = hn.at[src_tok].get(mode="promise_in_bounds")
    f_tiles = 2 if w_gate.shape[2] % (2 * LANE) == 0 else 1
    ys = _expert_call(blk_e, blk_valid, xs, w_gate, w_up, w_down, f_tiles)
    y0 = ys.at[dest[:n_tok]].get(mode="promise_in_bounds")
    y1 = ys.at[dest[n_tok:]].get(mode="promise_in_bounds")
    g0 = gate[0].reshape(n_tok, 1)
    g1 = gate[1].reshape(n_tok, 1)
    return _combine_call(h2, y0, y1, g0, g1, final_w, apply_final)


def _pad_cols(w, n):
    return jnp.concatenate([w, jnp.zeros((w.shape[0], n - w.shape[1]), w.dtype)], axis=1)


def _token_mix(h, norm_w, w_in, lb, hgrn_norm, s5, s5_d, s5_glu_w, s5_glu_b, dn_conv_w, dn_a_log, dn_dt_bias,
               dn_norm, gla_w_lr, gla_b_lr, gla_norm, w_branch, w_out, rs, tril):
    bsz, seq, d = h.shape
    m = MIX_W
    o = 0
    w_a = w_in[:, o:o + 4 * m]; o += 4 * m
    w_s5 = w_in[:, o:o + m]; o += m
    w_qkv = w_in[:, o:o + 3 * m]; o += 3 * m
    w_bd = w_in[:, o:o + 2 * DN_HEADS]; o += 2 * DN_HEADS
    w_dg = w_in[:, o:o + m]; o += m
    nqk = GLA_HEADS * GLA_DK
    w_gqk = w_in[:, o:o + 2 * nqk]; o += 2 * nqk
    w_gv = w_in[:, o:o + m]; o += m
    w_glr = w_in[:, o:o + GLA_RANK]; o += GLA_RANK
    w_gg = w_in[:, o:o + m]; o += m
    w_m = w_in[:, o:o + N_BRANCH * d]

    w_a = w_a.astype(BF16)
    w_c = jnp.concatenate([w_qkv, w_dg, _pad_cols(w_bd, LANE)], axis=1).astype(BF16)
    w_d = jnp.concatenate([w_gqk, w_gv, w_gg, _pad_cols(w_glr, LANE)], axis=1).astype(BF16)
    row = lambda a: a.reshape(1, -1).astype(F32)
    nw = row(norm_w)

    dummy = jnp.zeros((1, LANE), F32)
    y_a = _gla_call("hgrn", h, nw, w_a, row(lb), dummy, row(hgrn_norm), rs)

    wlr_pad = jnp.zeros((LANE, nqk), F32).at[:GLA_RANK].set(gla_w_lr)
    y_d = _gla_call("gla", h, nw, w_d, wlr_pad, row(gla_b_lr), row(gla_norm), rs)

    lane_row = lambda a: jnp.zeros((1, LANE), F32).at[0, DN_HEADS:2 * DN_HEADS].set(a)
    y_c = _delta_call(h, nw, w_c, dn_conv_w.astype(F32), lane_row(dn_a_log), lane_row(dn_dt_bias), row(dn_norm), tril)

    bre, bim, ar, ai, cre, cim = s5
    y_b = _s5_call(h, nw, w_s5.astype(BF16), bre, bim, ar, ai, cre, cim, row(s5_d), s5_glu_w.astype(BF16),
                   row(s5_glu_b))

    return _merge_call(h, nw, w_m.astype(BF16), y_a, y_b, y_c, y_d, w_branch.astype(BF16), w_out.astype(BF16))


def kernel(x, attn_norm, ffn_norm, final_norm, w_in, hgrn_lb, hgrn_norm, s5_a_re, s5_a_im, s5_log_dt, s5_b_re, s5_b_im, s5_c_re, s5_c_im, s5_d, s5_glu_w, s5_glu_b, dn_conv_w, dn_a_log, dn_dt_bias, dn_norm, gla_w_lr, gla_b_lr, gla_norm, w_branch, w_out, ffn_w_gate, ffn_w_up, ffn_w_down, moe_router, moe_w_gate, moe_w_up, moe_w_down):
    bsz, seq, d = x.shape
    depth = w_in.shape[0]
    assert bsz == SUBLANE, "the S5 kernel keeps one batch row per sublane"
    assert seq % CHUNK == 0 and d == D_MODEL
    lb_soft = jax.nn.softmax(hgrn_lb.astype(F32), axis=0)
    lb_all = jnp.cumsum(lb_soft, axis=0) - lb_soft[0]
    rs = jnp.asarray(_range_sum_matrix(), BF16)
    tril = jnp.asarray(np.tile(np.tril(np.ones((CHUNK, CHUNK), np.float32)), (1, 3)), BF16)
    row = lambda a: a.reshape(1, -1).astype(F32)

    h = x
    for layer in range(depth):
        s5 = _s5_params(s5_a_re[layer], s5_a_im[layer], s5_log_dt[layer], s5_b_re[layer], s5_b_im[layer],
                        s5_c_re[layer], s5_c_im[layer], bsz)
        h = _token_mix(h, attn_norm[layer], w_in[layer], lb_all[layer], hgrn_norm[layer], s5, s5_d[layer],
                       s5_glu_w[layer], s5_glu_b[layer], dn_conv_w[layer], dn_a_log[layer], dn_dt_bias[layer],
                       dn_norm[layer], gla_w_lr[layer], gla_b_lr[layer], gla_norm[layer], w_branch[layer],
                       w_out[layer], rs, tril)
        h2 = h.reshape(bsz * seq, d)
        j = layer // 2
        last = layer == depth - 1
        if layer % 2 == 0:
            f = ffn_w_gate.shape[2]
            f_tiles = 2 if f % (2 * LANE) == 0 else 1
            h2 = _ffn_call(h2, row(ffn_norm[layer]), ffn_w_gate[j].astype(BF16), ffn_w_up[j].astype(BF16),
                           ffn_w_down[j].astype(BF16), f_tiles)
            if last:
                h2 = _final_norm_call(h2, row(final_norm))
        else:
            h2 = _moe(h2, row(ffn_norm[layer]), moe_router[j], moe_w_gate[j].astype(BF16),
                      moe_w_up[j].astype(BF16), moe_w_down[j].astype(BF16), row(final_norm), last)
        h = h2.reshape(bsz, seq, d)
    return h
```

```python
import functools
import math

import numpy as np
import jax
import jax.numpy as jnp
from jax import lax
from jax.experimental import pallas as pl
from jax.experimental.pallas import tpu as pltpu

F32 = jnp.float32
BF16 = jnp.bfloat16
HIGHEST = lax.Precision.HIGHEST

D_MODEL = 1024
MIX_W = 512
HG_HEADS = 4
HG_DH = MIX_W // HG_HEADS
S5_GROUP = 16
S5_GROUPS = MIX_W // S5_GROUP
S5_STATE = 64
DN_HEADS = 4
DN_DH = MIX_W // DN_HEADS
CONV_K = 4
GLA_HEADS = 4
GLA_DK = 64
GLA_DV = MIX_W // GLA_HEADS
GLA_RANK = 16
GLA_NORMALIZER = 16.0
CHUNK = 64
N_BRANCH = 4
N_EXPERTS = 8
TOP_K = 2
NORM_EPS = 1e-6

LANE = 128
SUBLANE = 8
VMEM_LIMIT = 56 * 1024 * 1024

MIX_TILE = 256
S5_STEPS = 64
S5_CLUSTER = 8
ROW_TILE = 512
MOE_TILE = 512
LEVELS = (32, 16, 8, 4, 2, 1)


def _cparams(n_axes):
    return pltpu.CompilerParams(dimension_semantics=("arbitrary",) * n_axes, vmem_limit_bytes=VMEM_LIMIT)


def _rms(x, w):
    return x * lax.rsqrt(jnp.mean(x * x, axis=-1, keepdims=True) + NORM_EPS) * w


def _dot(a, b):
    return jnp.dot(a, b, preferred_element_type=F32)


def _dot_nt(a, b, precision=None):
    return lax.dot_general(a, b, (((1,), (1,)), ((), ())), preferred_element_type=F32, precision=precision)


def _dot_tn(a, b):
    return lax.dot_general(a, b, (((0,), (0,)), ((), ())), preferred_element_type=F32)


def _split3(x):
    hi = x.astype(BF16)
    r1 = x - hi.astype(F32)
    mid = r1.astype(BF16)
    lo = (r1 - mid.astype(F32)).astype(BF16)
    return hi, mid, lo


def _split2(x):
    hi = x.astype(BF16)
    return hi, (x - hi.astype(F32)).astype(BF16)


def _lhs3(hi, lo):
    return jnp.concatenate([hi, hi, lo], axis=1)


def _rhs3(hi, lo):
    return jnp.concatenate([hi, lo, hi], axis=0)


def _mm3(lhs3, rhs3):
    return jnp.dot(lhs3, rhs3, preferred_element_type=F32)


def _range_sum_matrix():
    c = CHUNK
    m = np.zeros((2 + len(LEVELS), c, c), np.float32)
    for r in range(c):
        m[0, r, :r + 1] = 1.0
        m[1, r, r + 1:] = 1.0
        for l, b in enumerate(LEVELS):
            mid = (r // (2 * b)) * 2 * b + b - 1
            if r > mid:
                m[2 + l, r, mid + 1:r + 1] = 1.0
            else:
                m[2 + l, r, r + 1:mid + 1] = 1.0
    m = m.reshape(-1, c)
    return np.concatenate([m, m, m], axis=1)


def _level_masks():
    i = lax.broadcasted_iota(jnp.int32, (CHUNK, CHUNK), 0)
    j = lax.broadcasted_iota(jnp.int32, (CHUNK, CHUNK), 1)
    masks = []
    for b in LEVELS:
        same = (i // (2 * b)) == (j // (2 * b))
        masks.append(same & ((i % (2 * b)) >= b) & ((j % (2 * b)) < b))
    return masks


def _head_norm_gate(o, w, gate):
    return o * lax.rsqrt(jnp.mean(o * o, axis=-1, keepdims=True) + NORM_EPS) * w * jax.nn.silu(gate)


def _gla_chunks(n_heads, dk, dv, q_ref, k_ref, v_ref, gs_ref, rs_ref, o_ref, st_ref, n_chunks):
    masks = _level_masks()
    chunks = range(n_chunks)
    heads = range(n_heads)
    rows = [slice(c * CHUNK, (c + 1) * CHUNK) for c in chunks]

    xs = []
    for c in chunks:
        g3 = jnp.concatenate([gs_ref[0, rows[c], :], gs_ref[1, rows[c], :], gs_ref[2, rows[c], :]], axis=0)
        xs.append(jnp.exp(_dot(rs_ref[...], g3)))

    scores, kv, qds, decs, diags = {}, {}, {}, {}, {}
    for c in chunks:
        qc, kc, vc = q_ref[rows[c], :], k_ref[rows[c], :], v_ref[rows[c], :]
        x = xs[c]
        for h in heads:
            cols = slice(h * dk, (h + 1) * dk)
            qh, kh = qc[:, cols], kc[:, cols]
            vh = vc[:, h * dv:(h + 1) * dv]
            a = jnp.zeros((CHUNK, CHUNK), F32)
            for l in range(len(LEVELS)):
                xl = x[(2 + l) * CHUNK:(3 + l) * CHUNK, cols]
                al = _dot_nt((qh * xl).astype(BF16), (kh * xl).astype(BF16))
                a = jnp.where(masks[l], al, a)
            scores[c, h] = a.astype(BF16)
            kv[c, h] = _dot_tn(vh.astype(BF16), (kh * x[CHUNK:2 * CHUNK, cols]).astype(BF16))
            qds[c, h] = (qh * x[0:CHUNK, cols]).astype(BF16)
            decs[c, h] = x[CHUNK - 1:CHUNK, cols]
            diags[c, h] = jnp.sum(qh * kh, axis=-1, keepdims=True) * vh

    for c in chunks:
        vc = v_ref[rows[c], :]
        for h in heads:
            vb = vc[:, h * dv:(h + 1) * dv].astype(BF16)
            o_ref[rows[c], h * dv:(h + 1) * dv] = _dot(scores[c, h], vb) + diags[c, h]

    for h in heads:
        st = st_ref[h]
        for c in chunks:
            o_ref[rows[c], h * dv:(h + 1) * dv] += _dot_nt(qds[c, h], st.astype(BF16))
            st = st * decs[c, h] + kv[c, h]
        st_ref[h] = st


def _gla_kernel(mode, h_ref, nw_ref, w_ref, aux1_ref, aux2_ref, hw_ref, rs_ref, y_ref,
                p_ref, q_ref, k_ref, gs_ref, o_ref, st_ref):
    if mode == "hgrn":
        n_heads, dk, dv = HG_HEADS, HG_DH, HG_DH
    else:
        n_heads, dk, dv = GLA_HEADS, GLA_DK, GLA_DV
    n = n_heads * dk
    tile = h_ref.shape[0]

    @pl.when(pl.program_id(1) == 0)
    def _():
        st_ref[...] = jnp.zeros_like(st_ref)

    xn = _rms(h_ref[...], nw_ref[...]).astype(BF16)
    p_ref[...] = _dot(xn, w_ref[...])
    if mode == "hgrn":
        lb = aux1_ref[...]
        z = p_ref[:, n:2 * n]
        g = jnp.log(lb + (1.0 - lb) * jax.nn.sigmoid(z))
        q_ref[...] = p_ref[:, 0:n] * dk ** -0.5
        k_ref[...] = (1.0 - lb) * jax.nn.sigmoid(-z)
        v_off = 2 * n
    else:
        v_off = 2 * n
        glr = p_ref[:, v_off + 2 * MIX_W:v_off + 2 * MIX_W + LANE]
        logit = jnp.dot(glr, aux1_ref[...], preferred_element_type=F32, precision=HIGHEST) + aux2_ref[...]
        g = jax.nn.log_sigmoid(logit) / GLA_NORMALIZER
        q_ref[...] = p_ref[:, 0:n] * dk ** -0.5
        k_ref[...] = p_ref[:, n:2 * n]
    hi, mid, lo = _split3(g)
    gs_ref[0] = hi
    gs_ref[1] = mid
    gs_ref[2] = lo

    _gla_chunks(n_heads, dk, dv, q_ref, k_ref, p_ref.at[:, v_off:v_off + MIX_W], gs_ref, rs_ref, o_ref, st_ref,
                tile // CHUNK)

    g_off = v_off + MIX_W
    for h in range(n_heads):
        cols = slice(h * dv, (h + 1) * dv)
        y = _head_norm_gate(o_ref[:, cols], hw_ref[...], p_ref[:, g_off + h * dv:g_off + (h + 1) * dv])
        y_ref[:, cols] = y.astype(y_ref.dtype)


def _gla_call(mode, h, norm_w, w, aux1, aux2, head_w, rs):
    bsz, seq, d = h.shape
    n_heads, dk, dv = (HG_HEADS, HG_DH, HG_DH) if mode == "hgrn" else (GLA_HEADS, GLA_DK, GLA_DV)
    n = n_heads * dk
    tile = min(MIX_TILE, seq)
    ncols = w.shape[1]
    full = lambda a: pl.BlockSpec(a.shape, lambda b, t: (0,) * a.ndim)
    return pl.pallas_call(
        functools.partial(_gla_kernel, mode),
        out_shape=jax.ShapeDtypeStruct((bsz, seq, MIX_W), BF16),
        grid=(bsz, seq // tile),
        in_specs=[pl.BlockSpec((None, tile, d), lambda b, t: (b, t, 0)),
                  full(norm_w), full(w), full(aux1), full(aux2), full(head_w), full(rs)],
        out_specs=pl.BlockSpec((None, tile, MIX_W), lambda b, t: (b, t, 0)),
        scratch_shapes=[pltpu.VMEM((tile, ncols), F32), pltpu.VMEM((tile, n), F32), pltpu.VMEM((tile, n), F32),
                        pltpu.VMEM((3, tile, n), BF16), pltpu.VMEM((tile, MIX_W), F32),
                        pltpu.VMEM((n_heads, dv, dk), F32)],
        compiler_params=_cparams(2),
        name="mix_" + mode,
    )(h, norm_w, w, aux1, aux2, head_w, rs)


def _delta_kernel(h_ref, nw_ref, w_ref, cw_ref, alog_ref, dtb_ref, hw_ref, tril_ref, y_ref,
                  p_ref, xs_ref, q_ref, k_ref, v_ref, sm_ref, o_ref, st_ref):
    tile = h_ref.shape[0]
    nh, dh = DN_HEADS, DN_DH
    qkv = 3 * MIX_W

    @pl.when(pl.program_id(1) == 0)
    def _():
        st_ref[...] = jnp.zeros_like(st_ref)
        xs_ref[0:SUBLANE, :] = jnp.zeros((SUBLANE, qkv), F32)

    xn = _rms(h_ref[...], nw_ref[...]).astype(BF16)
    p_ref[...] = _dot(xn, w_ref[...])

    xs_ref[SUBLANE:SUBLANE + tile, :] = p_ref[:, 0:qkv]
    conv = cw_ref[CONV_K - 1:CONV_K, :] * xs_ref[SUBLANE:SUBLANE + tile, :]
    for j in range(1, CONV_K):
        conv = conv + cw_ref[CONV_K - 1 - j:CONV_K - j, :] * xs_ref[SUBLANE - j:SUBLANE - j + tile, :]
    xs_ref[0:SUBLANE, :] = xs_ref[tile:tile + SUBLANE, :]
    conv = jax.nn.silu(conv)
    for h in range(nh):
        cols = slice(h * dh, (h + 1) * dh)
        qh = conv[:, h * dh:(h + 1) * dh]
        kh = conv[:, MIX_W + h * dh:MIX_W + (h + 1) * dh]
        q_ref[:, cols] = qh * lax.rsqrt(jnp.sum(qh * qh, axis=-1, keepdims=True) + NORM_EPS) * dh ** -0.5
        k_ref[:, cols] = kh * lax.rsqrt(jnp.sum(kh * kh, axis=-1, keepdims=True) + NORM_EPS)
    v_ref[...] = conv[:, 2 * MIX_W:3 * MIX_W]

    small = p_ref[:, qkv + MIX_W:qkv + MIX_W + LANE]
    gdec = -jnp.exp(alog_ref[...]) * jax.nn.softplus(small + dtb_ref[...])
    hi, mid, lo = _split3(gdec)
    sm_ref[0] = hi
    sm_ref[1] = mid
    sm_ref[2] = lo

    pw_ = nh * CHUNK
    ii = lax.broadcasted_iota(jnp.int32, (CHUNK, pw_), 0)
    lane_p = lax.broadcasted_iota(jnp.int32, (CHUNK, pw_), 1)
    jj = lane_p % CHUNK
    incl = ii >= jj
    strict = ii > jj
    eye = (ii == jj).astype(F32)
    head_p = (lane_p // CHUNK).astype(F32).astype(BF16)
    head_w = (lax.broadcasted_iota(jnp.int32, (CHUNK, MIX_W), 1) // dh).astype(F32).astype(BF16)
    sel_p = [head_p == h for h in range(nh)]
    sel_w = [head_w == h for h in range(nh)]
    n_chunks = tile // CHUNK

    def block_diag(y, sel):
        return jnp.concatenate([jnp.where(sel[h], y, jnp.zeros_like(y)) for h in range(nh)], axis=0)

    def head_cols(x, lo, width):
        return jnp.concatenate([jnp.broadcast_to(x[:, lo + h:lo + h + 1], (x.shape[0], width)) for h in range(nh)],
                               axis=1)

    gcums = []
    for c in range(n_chunks):
        rows = slice(c * CHUNK, (c + 1) * CHUNK)
        g3 = jnp.concatenate([sm_ref[0, rows, :], sm_ref[1, rows, :], sm_ref[2, rows, :]], axis=0)
        gcums.append(_dot(tril_ref[...], g3))
    prob = []
    for c in range(n_chunks):
        rows = slice(c * CHUNK, (c + 1) * CHUNK)
        gcum = gcums[c]
        gcum_t = gcum.T
        beta_c = jax.nn.sigmoid(p_ref[rows, qkv + MIX_W:qkv + MIX_W + LANE])
        qc, kc, vc = q_ref[rows, :], k_ref[rows, :], v_ref[rows, :]
        gcol_p = head_cols(gcum, nh, CHUNK)
        grow_p = jnp.concatenate([gcum_t[nh + h:nh + h + 1, :] for h in range(nh)], axis=1)
        decay = jnp.exp(jnp.where(incl, gcol_p - grow_p, -jnp.inf))
        gcol_w = head_cols(gcum, nh, dh)
        glast_w = gcol_w[CHUNK - 1:CHUNK, :]
        beta_w = head_cols(beta_c, 0, dh)
        egc = jnp.exp(gcol_w)
        kb = kc * beta_w
        k_bd = block_diag(kc.astype(BF16), sel_w)
        kk_qk = _dot_nt(jnp.concatenate([kb.astype(BF16), qc.astype(BF16)], axis=0), k_bd)
        lower = jnp.where(strict, kk_qk[0:CHUNK] * decay, 0.0)
        prob.append(dict(
            lower=lower, vb_bd=block_diag((vc * beta_w).astype(BF16), sel_w),
            kbg_bd=block_diag((kb * egc).astype(BF16), sel_w),
            qk=(kk_qk[CHUNK:2 * CHUNK] * decay).astype(BF16), q_dec=(qc * egc).astype(BF16),
            k_dec=(kc * jnp.exp(glast_w - gcol_w)).astype(BF16), a=jnp.exp(glast_w)))

    def rhs3_bd(hi, lo):
        bh, bl = block_diag(hi, sel_p), block_diag(lo, sel_p)
        return jnp.concatenate([bh, bl, bh], axis=0)

    t_inv = [eye - pr["lower"] for pr in prob]
    pw = []
    for pr in prob:
        hi, lo = _split2(pr["lower"])
        pw.append(_mm3(_lhs3(hi, lo), rhs3_bd(hi, lo)))
    for it in range(5):
        for i in range(n_chunks):
            hi, lo = _split2(pw[i])
            lhs_t = _lhs3(*_split2(t_inv[i]))
            if it < 4:
                both = _mm3(jnp.concatenate([lhs_t, _lhs3(hi, lo)], axis=0), rhs3_bd(hi, lo))
                t_inv[i] = t_inv[i] + both[0:CHUNK]
                pw[i] = both[CHUNK:2 * CHUNK]
            else:
                t_inv[i] = t_inv[i] + _mm3(lhs_t, rhs3_bd(hi, lo))
    for pr, t in zip(prob, t_inv):
        tb = t.astype(BF16)
        pr["u"] = _dot(tb, pr["vb_bd"])
        pr["w"] = _dot(tb, pr["kbg_bd"]).astype(BF16)

    st = [st_ref[h] for h in range(nh)]
    for c in range(n_chunks):
        rows = slice(c * CHUNK, (c + 1) * CHUNK)
        pr = prob[c]
        hcols = [slice(h * dh, (h + 1) * dh) for h in range(nh)]
        sb = [st[h].astype(BF16) for h in range(nh)]
        ws = [_dot(pr["w"][:, hcols[h]], sb[h]) for h in range(nh)]
        qs = [_dot(pr["q_dec"][:, hcols[h]], sb[h]) for h in range(nh)]
        vnb = [(pr["u"][:, hcols[h]] - ws[h]).astype(BF16) for h in range(nh)]
        for h in range(nh):
            o_ref[rows, hcols[h]] = qs[h] + _dot(pr["qk"][:, h * CHUNK:(h + 1) * CHUNK], vnb[h])
        for h in range(nh):
            st[h] = st[h] * pr["a"][:, hcols[h]] + _dot_tn(pr["k_dec"][:, hcols[h]], vnb[h])
    for h in range(nh):
        st_ref[h] = st[h]

    g_off = qkv
    for h in range(nh):
        cols = slice(h * dh, (h + 1) * dh)
        y = _head_norm_gate(o_ref[:, cols], hw_ref[...], p_ref[:, g_off + h * dh:g_off + (h + 1) * dh])
        y_ref[:, cols] = y.astype(y_ref.dtype)


def _delta_call(h, norm_w, w, conv_w, alog_row, dtb_row, head_w, tril):
    bsz, seq, d = h.shape
    tile = min(MIX_TILE, seq)
    ncols = w.shape[1]
    full = lambda a: pl.BlockSpec(a.shape, lambda b, t: (0,) * a.ndim)
    return pl.pallas_call(
        _delta_kernel,
        out_shape=jax.ShapeDtypeStruct((bsz, seq, MIX_W), BF16),
        grid=(bsz, seq // tile),
        in_specs=[pl.BlockSpec((None, tile, d), lambda b, t: (b, t, 0)),
                  full(norm_w), full(w), full(conv_w), full(alog_row), full(dtb_row), full(head_w), full(tril)],
        out_specs=pl.BlockSpec((None, tile, MIX_W), lambda b, t: (b, t, 0)),
        scratch_shapes=[pltpu.VMEM((tile, ncols), F32), pltpu.VMEM((tile + SUBLANE, 3 * MIX_W), F32),
                        pltpu.VMEM((tile, MIX_W), F32), pltpu.VMEM((tile, MIX_W), F32),
                        pltpu.VMEM((tile, MIX_W), F32), pltpu.VMEM((3, tile, LANE), BF16),
                        pltpu.VMEM((tile, MIX_W), F32), pltpu.VMEM((DN_HEADS, DN_DH, DN_DH), F32)],
        compiler_params=_cparams(2),
        name="mix_delta",
    )(h, norm_w, w, conv_w, alog_row, dtb_row, head_w, tril)


def _s5_kernel(h_ref, nw_ref, w_ref, bre_ref, bim_ref, ar_ref, ai_ref, cre_ref, cim_ref, d_ref, gw_ref, gb_ref,
               y_ref, u_ref, xr_ref, xi_ref, sr_ref, si_ref):
    bsz, steps, d = h_ref.shape
    rows = bsz * steps
    n_cl = bre_ref.shape[0]
    cw = bre_ref.shape[1]
    sw = bre_ref.shape[2]

    @pl.when(pl.program_id(0) == 0)
    def _():
        sr_ref[...] = jnp.zeros_like(sr_ref)
        si_ref[...] = jnp.zeros_like(si_ref)

    xn = _rms(h_ref[...], nw_ref[...]).astype(BF16).reshape(rows, d)
    p = _dot(xn, w_ref[...])
    n_lt = u_ref.shape[0]
    for b in range(bsz):
        for j in range(n_lt):
            u_ref[j, pl.ds(b, steps, stride=bsz), :] = p[b * steps:(b + 1) * steps, j * LANE:(j + 1) * LANE]

    u = jnp.concatenate([u_ref[j] for j in range(n_lt)], axis=1)
    ub = u.astype(BF16)
    for c in range(n_cl):
        uc = ub[:, c * cw:(c + 1) * cw]
        xr_ref[:, c * sw:(c + 1) * sw] = _dot(uc, bre_ref[c])
        xi_ref[:, c * sw:(c + 1) * sw] = _dot(uc, bim_ref[c])

    for c in range(n_cl):
        ls = slice(c * sw, (c + 1) * sw)
        ar, ai = ar_ref[:, ls], ai_ref[:, ls]

        def step(i, carry, ls=ls, ar=ar, ai=ai):
            xr, xi = carry
            rr = pl.ds(pl.multiple_of(i * bsz, bsz), bsz)
            nxr = ar * xr - ai * xi + xr_ref[rr, ls]
            nxi = ar * xi + ai * xr + xi_ref[rr, ls]
            xr_ref[rr, ls] = nxr
            xi_ref[rr, ls] = nxi
            return nxr, nxi

        xr, xi = lax.fori_loop(0, steps, step, (sr_ref[:, ls], si_ref[:, ls]), unroll=8)
        sr_ref[:, ls] = xr
        si_ref[:, ls] = xi

    ys = []
    for c in range(n_cl):
        ls = slice(c * sw, (c + 1) * sw)
        ys.append(_dot(xr_ref[:, ls].astype(BF16), cre_ref[c]) - _dot(xi_ref[:, ls].astype(BF16), cim_ref[c]))
    y = jnp.concatenate(ys, axis=1) + d_ref[...] * u
    z = jax.nn.gelu(y)
    gl = _dot(z.astype(BF16), gw_ref[...]) + gb_ref[...]
    out = z * jax.nn.sigmoid(gl)
    for j in range(n_lt):
        u_ref[j] = out[:, j * LANE:(j + 1) * LANE]
    for b in range(bsz):
        for j in range(n_lt):
            y_ref[b, :, j * LANE:(j + 1) * LANE] = u_ref[j, pl.ds(b, steps, stride=bsz), :].astype(y_ref.dtype)


def _s5_call(h, norm_w, w, bre, bim, ar, ai, cre, cim, d_row, glu_w, glu_b):
    bsz, seq, d = h.shape
    steps = min(S5_STEPS, seq)
    rows = steps * bsz
    n_state = ar.shape[1]
    full = lambda a: pl.BlockSpec(a.shape, lambda t: (0,) * a.ndim)
    return pl.pallas_call(
        _s5_kernel,
        out_shape=jax.ShapeDtypeStruct((bsz, seq, MIX_W), BF16),
        grid=(seq // steps,),
        in_specs=[pl.BlockSpec((bsz, steps, d), lambda t: (0, t, 0)), full(norm_w), full(w),
                  full(bre), full(bim), full(ar), full(ai), full(cre), full(cim), full(d_row), full(glu_w),
                  full(glu_b)],
        out_specs=pl.BlockSpec((bsz, steps, MIX_W), lambda t: (0, t, 0)),
        scratch_shapes=[pltpu.VMEM((MIX_W // LANE, rows, LANE), F32),
                        pltpu.VMEM((rows, n_state), F32), pltpu.VMEM((rows, n_state), F32),
                        pltpu.VMEM((bsz, n_state), F32), pltpu.VMEM((bsz, n_state), F32)],
        compiler_params=_cparams(1),
        name="mix_s5",
    )(h, norm_w, w, bre, bim, ar, ai, cre, cim, d_row, glu_w, glu_b)


def _s5_params(a_re, a_im, log_dt, b_re, b_im, c_re, c_im, bsz):
    dt = jnp.exp(log_dt)[:, None]
    mag = jnp.exp(a_re * dt)
    abar_r, abar_i = mag * jnp.cos(a_im * dt), mag * jnp.sin(a_im * dt)
    den = a_re * a_re + a_im * a_im
    nr, ni = abar_r - 1.0, abar_i
    zr, zi = (nr * a_re + ni * a_im) / den, (ni * a_re - nr * a_im) / den
    bbar_r = zr[..., None] * b_re - zi[..., None] * b_im
    bbar_i = zr[..., None] * b_im + zi[..., None] * b_re
    n_cl = S5_GROUPS // S5_CLUSTER
    eye = jnp.eye(S5_CLUSTER, dtype=F32)

    def pack_b(bb):
        bb = bb.reshape(n_cl, S5_CLUSTER, S5_STATE, S5_GROUP)
        out = jnp.einsum("cgph,gk->cghkp", bb, eye)
        return out.reshape(n_cl, S5_CLUSTER * S5_GROUP, S5_CLUSTER * S5_STATE).astype(BF16)

    def pack_c(cc):
        cc = cc.reshape(n_cl, S5_CLUSTER, S5_GROUP, S5_STATE)
        out = jnp.einsum("cghp,gk->cgpkh", cc, eye)
        return out.reshape(n_cl, S5_CLUSTER * S5_STATE, S5_CLUSTER * S5_GROUP).astype(BF16)

    bc = lambda a: jnp.broadcast_to(a.reshape(1, -1), (bsz, S5_GROUPS * S5_STATE))
    return pack_b(bbar_r), pack_b(bbar_i), bc(abar_r), bc(abar_i), pack_c(c_re), pack_c(c_im)


def _merge_kernel(h_ref, nw_ref, wm_ref, ya_ref, yb_ref, yc_ref, yd_ref, wb_ref, wo_ref, o_ref):
    h = h_ref[...]
    xn = _rms(h, nw_ref[...]).astype(BF16)
    d = h.shape[1]
    mixed = jnp.zeros(h.shape, F32)
    for n, y_ref in enumerate((ya_ref, yb_ref, yc_ref, yd_ref)):
        gate = jax.nn.sigmoid(_dot(xn, wm_ref[:, n * d:(n + 1) * d]))
        mixed = mixed + gate * _dot(y_ref[...], wb_ref[n])
    o_ref[...] = h + _dot(mixed.astype(BF16), wo_ref[...])


def _merge_call(h, norm_w, w_m, y_a, y_b, y_c, y_d, w_b, w_o):
    bsz, seq, d = h.shape
    tile = min(ROW_TILE, seq)
    full = lambda a: pl.BlockSpec(a.shape, lambda b, t: (0,) * a.ndim)
    ytile = pl.BlockSpec((None, tile, MIX_W), lambda b, t: (b, t, 0))
    return pl.pallas_call(
        _merge_kernel,
        out_shape=jax.ShapeDtypeStruct(h.shape, F32),
        grid=(bsz, seq // tile),
        in_specs=[pl.BlockSpec((None, tile, d), lambda b, t: (b, t, 0)), full(norm_w), full(w_m),
                  ytile, ytile, ytile, ytile, full(w_b), full(w_o)],
        out_specs=pl.BlockSpec((None, tile, d), lambda b, t: (b, t, 0)),
        compiler_params=_cparams(2),
        name="merge",
    )(h, norm_w, w_m, y_a, y_b, y_c, y_d, w_b, w_o)


def _ffn_kernel(h_ref, nw_ref, wg_ref, wu_ref, wd_ref, o_ref, hn_ref, acc_ref):
    j = pl.program_id(1)

    @pl.when(j == 0)
    def _():
        hn_ref[...] = _rms(h_ref[...], nw_ref[...]).astype(BF16)
        acc_ref[...] = jnp.zeros_like(acc_ref)

    hn = hn_ref[...]
    act = (jax.nn.silu(_dot(hn, wg_ref[...])) * _dot(hn, wu_ref[...])).astype(BF16)
    acc_ref[...] += _dot(act, wd_ref[...])

    @pl.when(j == pl.num_programs(1) - 1)
    def _():
        o_ref[...] = h_ref[...] + acc_ref[...]


def _ffn_call(h2, norm_w, w_gate, w_up, w_down, f_tiles):
    n_tok, d = h2.shape
    f = w_gate.shape[1]
    tf = f // f_tiles
    tile = min(ROW_TILE, n_tok)
    return pl.pallas_call(
        _ffn_kernel,
        out_shape=jax.ShapeDtypeStruct(h2.shape, F32),
        grid=(n_tok // tile, f_tiles),
        in_specs=[pl.BlockSpec((tile, d), lambda i, j: (i, 0)),
                  pl.BlockSpec(norm_w.shape, lambda i, j: (0, 0)),
                  pl.BlockSpec((d, tf), lambda i, j: (0, j)),
                  pl.BlockSpec((d, tf), lambda i, j: (0, j)),
                  pl.BlockSpec((tf, d), lambda i, j: (j, 0))],
        out_specs=pl.BlockSpec((tile, d), lambda i, j: (i, 0)),
        scratch_shapes=[pltpu.VMEM((tile, d), BF16), pltpu.VMEM((tile, d), F32)],
        compiler_params=_cparams(2),
        name="ffn_dense",
    )(h2, norm_w, w_gate, w_up, w_down)


def _route_kernel(h_ref, nw_ref, rt_ref, hn_ref, idx_ref, gate_ref):
    hn = _rms(h_ref[...], nw_ref[...])
    hn_ref[...] = hn.astype(BF16)
    logits = jnp.dot(hn, rt_ref[...], preferred_element_type=F32, precision=HIGHEST)
    lg = logits.T[0:N_EXPERTS, :]
    row = lax.broadcasted_iota(jnp.int32, lg.shape, 0)
    m1 = jnp.max(lg, axis=0, keepdims=True)
    i1 = jnp.min(jnp.where(lg == m1, row, N_EXPERTS), axis=0, keepdims=True)
    lg2 = jnp.where(row == i1, -jnp.inf, lg)
    m2 = jnp.max(lg2, axis=0, keepdims=True)
    i2 = jnp.min(jnp.where(lg2 == m2, row, N_EXPERTS), axis=0, keepdims=True)
    e2 = jnp.exp(m2 - m1)
    g1 = 1.0 / (1.0 + e2)
    g2 = e2 / (1.0 + e2)
    idx_ref[...] = jnp.where(row == 0, i1, jnp.where(row == 1, i2, 0))
    gate_ref[...] = jnp.where(row == 0, g1, jnp.where(row == 1, g2, 0.0))


def _route_call(h2, norm_w, router_pad):
    n_tok, d = h2.shape
    tile = min(ROW_TILE, n_tok)
    return pl.pallas_call(
        _route_kernel,
        out_shape=(jax.ShapeDtypeStruct((n_tok, d), BF16),
                   jax.ShapeDtypeStruct((N_EXPERTS, n_tok), jnp.int32),
                   jax.ShapeDtypeStruct((N_EXPERTS, n_tok), F32)),
        grid=(n_tok // tile,),
        in_specs=[pl.BlockSpec((tile, d), lambda i: (i, 0)),
                  pl.BlockSpec(norm_w.shape, lambda i: (0, 0)),
                  pl.BlockSpec(router_pad.shape, lambda i: (0, 0))],
        out_specs=(pl.BlockSpec((tile, d), lambda i: (i, 0)),
                   pl.BlockSpec((N_EXPERTS, tile), lambda i: (0, i)),
                   pl.BlockSpec((N_EXPERTS, tile), lambda i: (0, i))),
        compiler_params=_cparams(1),
        name="moe_route",
    )(h2, norm_w, router_pad)


def _expert_kernel(be_ref, bv_ref, x_ref, wg_ref, wu_ref, wd_ref, o_ref, acc_ref):
    i, j = pl.program_id(0), pl.program_id(1)
    last = pl.num_programs(1) - 1

    @pl.when(j == 0)
    def _():
        acc_ref[...] = jnp.zeros_like(acc_ref)

    @pl.when(bv_ref[i] != 0)
    def _():
        x = x_ref[...]
        act = (jax.nn.silu(_dot(x, wg_ref[...])) * _dot(x, wu_ref[...])).astype(BF16)
        acc_ref[...] += _dot(act, wd_ref[...])

    @pl.when(j == last)
    def _():
        o_ref[...] = acc_ref[...].astype(o_ref.dtype)


def _expert_call(blk_e, blk_valid, xs, w_gate, w_up, w_down, f_tiles):
    n_rows, d = xs.shape
    f = w_gate.shape[2]
    tf = f // f_tiles
    n_blk = n_rows // MOE_TILE
    grid_spec = pltpu.PrefetchScalarGridSpec(
        num_scalar_prefetch=2,
        grid=(n_blk, f_tiles),
        in_specs=[pl.BlockSpec((MOE_TILE, d), lambda i, j, be, bv: (i, 0)),
                  pl.BlockSpec((None, d, tf), lambda i, j, be, bv: (be[i], 0, j)),
                  pl.BlockSpec((None, d, tf), lambda i, j, be, bv: (be[i], 0, j)),
                  pl.BlockSpec((None, tf, d), lambda i, j, be, bv: (be[i], j, 0))],
        out_specs=pl.BlockSpec((MOE_TILE, d), lambda i, j, be, bv: (i, 0)),
        scratch_shapes=[pltpu.VMEM((MOE_TILE, d), F32)],
    )
    return pl.pallas_call(
        _expert_kernel,
        out_shape=jax.ShapeDtypeStruct((n_rows, d), BF16),
        grid_spec=grid_spec,
        compiler_params=_cparams(2),
        name="moe_experts",
    )(blk_e, blk_valid, xs, w_gate, w_up, w_down)


def _combine_kernel(apply_norm, h_ref, y0_ref, y1_ref, g0_ref, g1_ref, nw_ref, o_ref):
    out = h_ref[...] + (g0_ref[...] * y0_ref[...].astype(F32) + g1_ref[...] * y1_ref[...].astype(F32))
    if apply_norm:
        out = _rms(out, nw_ref[...])
    o_ref[...] = out


def _combine_call(h2, y0, y1, g0, g1, norm_w, apply_norm):
    n_tok, d = h2.shape
    tile = min(ROW_TILE, n_tok)
    row = pl.BlockSpec((tile, d), lambda i: (i, 0))
    col = pl.BlockSpec((tile, 1), lambda i: (i, 0))
    return pl.pallas_call(
        functools.partial(_combine_kernel, apply_norm),
        out_shape=jax.ShapeDtypeStruct(h2.shape, F32),
        grid=(n_tok // tile,),
        in_specs=[row, row, row, col, col, pl.BlockSpec(norm_w.shape, lambda i: (0, 0))],
        out_specs=row,
        compiler_params=_cparams(1),
        name="moe_combine",
    )(h2, y0, y1, g0, g1, norm_w)


def _final_norm_kernel(h_ref, nw_ref, o_ref):
    o_ref[...] = _rms(h_ref[...], nw_ref[...])


def _final_norm_call(h2, norm_w):
    n_tok, d = h2.shape
    tile = min(ROW_TILE, n_tok)
    row = pl.BlockSpec((tile, d), lambda i: (i, 0))
    return pl.pallas_call(
        _final_norm_kernel,
        out_shape=jax.ShapeDtypeStruct(h2.shape, F32),
        grid=(n_tok // tile,),
        in_specs=[row, pl.BlockSpec(norm_w.shape, lambda i: (0, 0))],
        out_specs=row,
        compiler_params=_cparams(1),
        name="final_norm",
    )(h2, norm_w)


def _moe(h2, norm_w, router, w_gate, w_up, w_down, final_w, apply_final):
    n_tok, d = h2.shape
    router_pad = jnp.zeros((d, LANE), F32).at[:, :N_EXPERTS].set(router)
    hn, idx, gate = _route_call(h2, norm_w, router_pad)
    e_flat = idx[:TOP_K].reshape(-1)
    tok_flat = jnp.tile(jnp.arange(n_tok, dtype=jnp.int32), TOP_K)
    onehot = (e_flat[:, None] == jnp.arange(N_EXPERTS, dtype=jnp.int32)[None, :]).astype(jnp.int32)
    csum = jnp.cumsum(onehot, axis=0)
    rank = jnp.sum(csum * onehot, axis=1) - 1
    counts = csum[-1]
    padded = (counts + MOE_TILE - 1) // MOE_TILE * MOE_TILE
    pad_end = jnp.cumsum(padded)
    pad_start = pad_end - padded
    dest = pad_start[e_flat] + rank
    n_assign = n_tok * TOP_K
    n_blk = -(-n_assign // MOE_TILE) + N_EXPERTS
    src_tok = jnp.zeros((n_blk * MOE_TILE,), jnp.int32).at[dest].set(
        tok_flat, mode="promise_in_bounds", unique_indices=True)
    blk_start = jnp.arange(n_blk, dtype=jnp.int32) * MOE_TILE
    blk_e = jnp.minimum(jnp.searchsorted(pad_end, blk_start, side="right"), N_EXPERTS - 1).astype(jnp.int32)
    blk_valid = (blk_start < pad_end[-1]).astype(jnp.int32)
    xs = hn.at[src_tok].get(mode="promise_in_bounds")
    f_tiles = 2 if w_gate.shape[2] % (2 * LANE) == 0 else 1
    ys = _expert_call(blk_e, blk_valid, xs, w_gate, w_up, w_down, f_tiles)
    y0 = ys.at[dest[:n_tok]].get(mode="promise_in_bounds")
    y1 = ys.at[dest[n_tok:]].get(mode="promise_in_bounds")
    g0 = gate[0].reshape(n_tok, 1)
    g1 = gate[1].reshape(n_tok, 1)
    return _combine_call(h2, y0, y1, g0, g1, final_w, apply_final)


def _pad_cols(w, n):
    return jnp.concatenate([w, jnp.zeros((w.shape[0], n - w.shape[1]), w.dtype)], axis=1)


def _token_mix(h, norm_w, w_in, lb, hgrn_norm, s5, s5_d, s5_glu_w, s5_glu_b, dn_conv_w, dn_a_log, dn_dt_bias,
               dn_norm, gla_w_lr, gla_b_lr, gla_norm, w_branch, w_out, rs, tril):
    bsz, seq, d = h.shape
    m = MIX_W
    o = 0
    w_a = w_in[:, o:o + 4 * m]; o += 4 * m
    w_s5 = w_in[:, o:o + m]; o += m
    w_qkv = w_in[:, o:o + 3 * m]; o += 3 * m
    w_bd = w_in[:, o:o + 2 * DN_HEADS]; o += 2 * DN_HEADS
    w_dg = w_in[:, o:o + m]; o += m
    nqk = GLA_HEADS * GLA_DK
    w_gqk = w_in[:, o:o + 2 * nqk]; o += 2 * nqk
    w_gv = w_in[:, o:o + m]; o += m
    w_glr = w_in[:, o:o + GLA_RANK]; o += GLA_RANK
    w_gg = w_in[:, o:o + m]; o += m
    w_m = w_in[:, o:o + N_BRANCH * d]

    w_a = w_a.astype(BF16)
    w_c = jnp.concatenate([w_qkv, w_dg, _pad_cols(w_bd, LANE)], axis=1).astype(BF16)
    w_d = jnp.concatenate([w_gqk, w_gv, w_gg, _pad_cols(w_glr, LANE)], axis=1).astype(BF16)
    row = lambda a: a.reshape(1, -1).astype(F32)
    nw = row(norm_w)

    dummy = jnp.zeros((1, LANE), F32)
    y_a = _gla_call("hgrn", h, nw, w_a, row(lb), dummy, row(hgrn_norm), rs)

    wlr_pad = jnp.zeros((LANE, nqk), F32).at[:GLA_RANK].set(gla_w_lr)
    y_d = _gla_call("gla", h, nw, w_d, wlr_pad, row(gla_b_lr), row(gla_norm), rs)

    lane_row = lambda a: jnp.zeros((1, LANE), F32).at[0, DN_HEADS:2 * DN_HEADS].set(a)
    y_c = _delta_call(h, nw, w_c, dn_conv_w.astype(F32), lane_row(dn_a_log), lane_row(dn_dt_bias), row(dn_norm), tril)

    bre, bim, ar, ai, cre, cim = s5
    y_b = _s5_call(h, nw, w_s5.astype(BF16), bre, bim, ar, ai, cre, cim, row(s5_d), s5_glu_w.astype(BF16),
                   row(s5_glu_b))

    return _merge_call(h, nw, w_m.astype(BF16), y_a, y_b, y_c, y_d, w_branch.astype(BF16), w_out.astype(BF16))


def kernel(x, attn_norm, ffn_norm, final_norm, w_in, hgrn_lb, hgrn_norm, s5_a_re, s5_a_im, s5_log_dt, s5_b_re, s5_b_im, s5_c_re, s5_c_im, s5_d, s5_glu_w, s5_glu_b, dn_conv_w, dn_a_log, dn_dt_bias, dn_norm, gla_w_lr, gla_b_lr, gla_norm, w_branch, w_out, ffn_w_gate, ffn_w_up, ffn_w_down, moe_router, moe_w_gate, moe_w_up, moe_w_down):
    bsz, seq, d = x.shape
    depth = w_in.shape[0]
    assert bsz == SUBLANE, "the S5 kernel keeps one batch row per sublane"
    assert seq % CHUNK == 0 and d == D_MODEL
    lb_soft = jax.nn.softmax(hgrn_lb.astype(F32), axis=0)
    lb_all = jnp.cumsum(lb_soft, axis=0) - lb_soft[0]
    rs = jnp.asarray(_range_sum_matrix(), BF16)
    tril = jnp.asarray(np.tile(np.tril(np.ones((CHUNK, CHUNK), np.float32)), (1, 3)), BF16)
    row = lambda a: a.reshape(1, -1).astype(F32)

    h = x
    for layer in range(depth):
        s5 = _s5_params(s5_a_re[layer], s5_a_im[layer], s5_log_dt[layer], s5_b_re[layer], s5_b_im[layer],
                        s5_c_re[layer], s5_c_im[layer], bsz)
        h = _token_mix(h, attn_norm[layer], w_in[layer], lb_all[layer], hgrn_norm[layer], s5, s5_d[layer],
                       s5_glu_w[layer], s5_glu_b[layer], dn_conv_w[layer], dn_a_log[layer], dn_dt_bias[layer],
                       dn_norm[layer], gla_w_lr[layer], gla_b_lr[layer], gla_norm[layer], w_branch[layer],
                       w_out[layer], rs, tril)
        h2 = h.reshape(bsz * seq, d)
        j = layer // 2
        last = layer == depth - 1
        if layer % 2 == 0:
            f = ffn_w_gate.shape[2]
            f_tiles = 2 if f % (2 * LANE) == 0 else 1
            h2 = _ffn_call(h2, row(ffn_norm[layer]), ffn_w_gate[j].astype(BF16), ffn_w_up[j].astype(BF16),
                           ffn_w_down[j].astype(BF16), f_tiles)
            if last:
                h2 = _final_norm_call(h2, row(final_norm))
        else:
            h2 = _moe(h2, row(ffn_norm[layer]), moe_router[j], moe_w_gate[j].astype(BF16),
                      moe_w_up[j].astype(BF16), moe_w_down[j].astype(BF16), row(final_norm), last)
        h = h2.reshape(bsz, seq, d)
    return h
```

```python
import functools
import math

import numpy as np
import jax
import jax.numpy as jnp
from jax import lax
from jax.experimental import pallas as pl
from jax.experimental.pallas import tpu as pltpu

F32 = jnp.float32
BF16 = jnp.bfloat16
HIGHEST = lax.Precision.HIGHEST

D_MODEL = 1024
MIX_W = 512
HG_HEADS = 4
HG_DH = MIX_W // HG_HEADS
S5_GROUP = 16
S5_GROUPS = MIX_W // S5_GROUP
S5_STATE = 64
DN_HEADS = 4
DN_DH = MIX_W // DN_HEADS
CONV_K = 4
GLA_HEADS = 4
GLA_DK = 64
GLA_DV = MIX_W // GLA_HEADS
GLA_RANK = 16
GLA_NORMALIZER = 16.0
CHUNK = 64
N_BRANCH = 4
N_EXPERTS = 8
TOP_K = 2
NORM_EPS = 1e-6

LANE = 128
SUBLANE = 8
VMEM_LIMIT = 56 * 1024 * 1024

MIX_TILE = 512
S5_STEPS = 128
S5_CLUSTER = 8
ROW_TILE = 512
MOE_TILE = 512
LEVELS = (32, 16, 8, 4, 2, 1)


def _cparams(n_axes):
    return pltpu.CompilerParams(dimension_semantics=("arbitrary",) * n_axes, vmem_limit_bytes=VMEM_LIMIT)


def _rms(x, w):
    return x * lax.rsqrt(jnp.mean(x * x, axis=-1, keepdims=True) + NORM_EPS) * w


def _dot(a, b):
    return jnp.dot(a, b, preferred_element_type=F32)


def _dot_nt(a, b, precision=None):
    return lax.dot_general(a, b, (((1,), (1,)), ((), ())), preferred_element_type=F32, precision=precision)


def _dot_tn(a, b):
    return lax.dot_general(a, b, (((0,), (0,)), ((), ())), preferred_element_type=F32)


def _split3(x):
    hi = x.astype(BF16)
    r1 = x - hi.astype(F32)
    mid = r1.astype(BF16)
    lo = (r1 - mid.astype(F32)).astype(BF16)
    return hi, mid, lo


def _split2(x):
    hi = x.astype(BF16)
    return hi, (x - hi.astype(F32)).astype(BF16)


def _lhs3(hi, lo):
    return jnp.concatenate([hi, hi, lo], axis=1)


def _rhs3(hi, lo):
    return jnp.concatenate([hi, lo, hi], axis=0)


def _mm3(lhs3, rhs3):
    return jnp.dot(lhs3, rhs3, preferred_element_type=F32)


def _range_sum_matrix():
    c = CHUNK
    m = np.zeros((2 + len(LEVELS), c, c), np.float32)
    for r in range(c):
        m[0, r, :r + 1] = 1.0
        m[1, r, r + 1:] = 1.0
        for l, b in enumerate(LEVELS):
            mid = (r // (2 * b)) * 2 * b + b - 1
            if r > mid:
                m[2 + l, r, mid + 1:r + 1] = 1.0
            else:
                m[2 + l, r, r + 1:mid + 1] = 1.0
    m = m.reshape(-1, c)
    return np.concatenate([m, m, m], axis=1)


def _level_masks():
    i = lax.broadcasted_iota(jnp.int32, (CHUNK, CHUNK), 0)
    j = lax.broadcasted_iota(jnp.int32, (CHUNK, CHUNK), 1)
    masks = []
    for b in LEVELS:
        same = (i // (2 * b)) == (j // (2 * b))
        masks.append(same & ((i % (2 * b)) >= b) & ((j % (2 * b)) < b))
    return masks


def _head_norm_gate(o, w, gate):
    return o * lax.rsqrt(jnp.mean(o * o, axis=-1, keepdims=True) + NORM_EPS) * w * jax.nn.silu(gate)


def _gla_chunks(n_heads, dk, dv, q_ref, k_ref, v_ref, gs_ref, rs_ref, o_ref, st_ref, n_chunks):
    masks = _level_masks()
    chunks = range(n_chunks)
    heads = range(n_heads)
    rows = [slice(c * CHUNK, (c + 1) * CHUNK) for c in chunks]

    xs = []
    for c in chunks:
        g3 = jnp.concatenate([gs_ref[0, rows[c], :], gs_ref[1, rows[c], :], gs_ref[2, rows[c], :]], axis=0)
        xs.append(jnp.exp(_dot(rs_ref[...], g3)))

    scores, kv, qds, decs, diags = {}, {}, {}, {}, {}
    for c in chunks:
        qc, kc, vc = q_ref[rows[c], :], k_ref[rows[c], :], v_ref[rows[c], :]
        x = xs[c]
        for h in heads:
            cols = slice(h * dk, (h + 1) * dk)
            qh, kh = qc[:, cols], kc[:, cols]
            vh = vc[:, h * dv:(h + 1) * dv]
            a = jnp.zeros((CHUNK, CHUNK), F32)
            for l in range(len(LEVELS)):
                xl = x[(2 + l) * CHUNK:(3 + l) * CHUNK, cols]
                al = _dot_nt((qh * xl).astype(BF16), (kh * xl).astype(BF16))
                a = jnp.where(masks[l], al, a)
            scores[c, h] = a.astype(BF16)
            kv[c, h] = _dot_tn(vh.astype(BF16), (kh * x[CHUNK:2 * CHUNK, cols]).astype(BF16))
            qds[c, h] = (qh * x[0:CHUNK, cols]).astype(BF16)
            decs[c, h] = x[CHUNK - 1:CHUNK, cols]
            diags[c, h] = jnp.sum(qh * kh, axis=-1, keepdims=True) * vh

    for c in chunks:
        vc = v_ref[rows[c], :]
        for h in heads:
            vb = vc[:, h * dv:(h + 1) * dv].astype(BF16)
            o_ref[rows[c], h * dv:(h + 1) * dv] = _dot(scores[c, h], vb) + diags[c, h]

    for h in heads:
        st = st_ref[h]
        for c in chunks:
            o_ref[rows[c], h * dv:(h + 1) * dv] += _dot_nt(qds[c, h], st.astype(BF16))
            st = st * decs[c, h] + kv[c, h]
        st_ref[h] = st


def _gla_kernel(mode, h_ref, nw_ref, w_ref, aux1_ref, aux2_ref, hw_ref, rs_ref, y_ref,
                p_ref, q_ref, k_ref, gs_ref, o_ref, st_ref):
    if mode == "hgrn":
        n_heads, dk, dv = HG_HEADS, HG_DH, HG_DH
    else:
        n_heads, dk, dv = GLA_HEADS, GLA_DK, GLA_DV
    n = n_heads * dk
    tile = h_ref.shape[0]

    @pl.when(pl.program_id(1) == 0)
    def _():
        st_ref[...] = jnp.zeros_like(st_ref)

    xn = _rms(h_ref[...], nw_ref[...]).astype(BF16)
    p_ref[...] = _dot(xn, w_ref[...])
    if mode == "hgrn":
        lb = aux1_ref[...]
        z = p_ref[:, n:2 * n]
        g = jnp.log(lb + (1.0 - lb) * jax.nn.sigmoid(z))
        q_ref[...] = p_ref[:, 0:n] * dk ** -0.5
        k_ref[...] = (1.0 - lb) * jax.nn.sigmoid(-z)
        v_off = 2 * n
    else:
        v_off = 2 * n
        glr = p_ref[:, v_off + 2 * MIX_W:v_off + 2 * MIX_W + LANE]
        logit = jnp.dot(glr, aux1_ref[...], preferred_element_type=F32, precision=HIGHEST) + aux2_ref[...]
        g = jax.nn.log_sigmoid(logit) / GLA_NORMALIZER
        q_ref[...] = p_ref[:, 0:n] * dk ** -0.5
        k_ref[...] = p_ref[:, n:2 * n]
    hi, mid, lo = _split3(g)
    gs_ref[0] = hi
    gs_ref[1] = mid
    gs_ref[2] = lo

    _gla_chunks(n_heads, dk, dv, q_ref, k_ref, p_ref.at[:, v_off:v_off + MIX_W], gs_ref, rs_ref, o_ref, st_ref,
                tile // CHUNK)

    g_off = v_off + MIX_W
    for h in range(n_heads):
        cols = slice(h * dv, (h + 1) * dv)
        y = _head_norm_gate(o_ref[:, cols], hw_ref[...], p_ref[:, g_off + h * dv:g_off + (h + 1) * dv])
        y_ref[:, cols] = y.astype(y_ref.dtype)


def _gla_call(mode, h, norm_w, w, aux1, aux2, head_w, rs):
    bsz, seq, d = h.shape
    n_heads, dk, dv = (HG_HEADS, HG_DH, HG_DH) if mode == "hgrn" else (GLA_HEADS, GLA_DK, GLA_DV)
    n = n_heads * dk
    tile = min(MIX_TILE, seq)
    ncols = w.shape[1]
    full = lambda a: pl.BlockSpec(a.shape, lambda b, t: (0,) * a.ndim)
    return pl.pallas_call(
        functools.partial(_gla_kernel, mode),
        out_shape=jax.ShapeDtypeStruct((bsz, seq, MIX_W), BF16),
        grid=(bsz, seq // tile),
        in_specs=[pl.BlockSpec((None, tile, d), lambda b, t: (b, t, 0)),
                  full(norm_w), full(w), full(aux1), full(aux2), full(head_w), full(rs)],
        out_specs=pl.BlockSpec((None, tile, MIX_W), lambda b, t: (b, t, 0)),
        scratch_shapes=[pltpu.VMEM((tile, ncols), F32), pltpu.VMEM((tile, n), F32), pltpu.VMEM((tile, n), F32),
                        pltpu.VMEM((3, tile, n), BF16), pltpu.VMEM((tile, MIX_W), F32),
                        pltpu.VMEM((n_heads, dv, dk), F32)],
        compiler_params=_cparams(2),
        name="mix_" + mode,
    )(h, norm_w, w, aux1, aux2, head_w, rs)


def _delta_kernel(h_ref, nw_ref, w_ref, cw_ref, alog_ref, dtb_ref, hw_ref, tril_ref, y_ref,
                  p_ref, xs_ref, q_ref, k_ref, v_ref, sm_ref, o_ref, st_ref):
    tile = h_ref.shape[0]
    nh, dh = DN_HEADS, DN_DH
    qkv = 3 * MIX_W

    @pl.when(pl.program_id(1) == 0)
    def _():
        st_ref[...] = jnp.zeros_like(st_ref)
        xs_ref[0:SUBLANE, :] = jnp.zeros((SUBLANE, qkv), F32)

    xn = _rms(h_ref[...], nw_ref[...]).astype(BF16)
    p_ref[...] = _dot(xn, w_ref[...])

    xs_ref[SUBLANE:SUBLANE + tile, :] = p_ref[:, 0:qkv]
    conv = cw_ref[CONV_K - 1:CONV_K, :] * xs_ref[SUBLANE:SUBLANE + tile, :]
    for j in range(1, CONV_K):
        conv = conv + cw_ref[CONV_K - 1 - j:CONV_K - j, :] * xs_ref[SUBLANE - j:SUBLANE - j + tile, :]
    xs_ref[0:SUBLANE, :] = xs_ref[tile:tile + SUBLANE, :]
    conv = jax.nn.silu(conv)
    for h in range(nh):
        cols = slice(h * dh, (h + 1) * dh)
        qh = conv[:, h * dh:(h + 1) * dh]
        kh = conv[:, MIX_W + h * dh:MIX_W + (h + 1) * dh]
        q_ref[:, cols] = qh * lax.rsqrt(jnp.sum(qh * qh, axis=-1, keepdims=True) + NORM_EPS) * dh ** -0.5
        k_ref[:, cols] = kh * lax.rsqrt(jnp.sum(kh * kh, axis=-1, keepdims=True) + NORM_EPS)
    v_ref[...] = conv[:, 2 * MIX_W:3 * MIX_W]

    small = p_ref[:, qkv + MIX_W:qkv + MIX_W + LANE]
    gdec = -jnp.exp(alog_ref[...]) * jax.nn.softplus(small + dtb_ref[...])
    hi, mid, lo = _split3(gdec)
    sm_ref[0] = hi
    sm_ref[1] = mid
    sm_ref[2] = lo

    pw_ = nh * CHUNK
    ii = lax.broadcasted_iota(jnp.int32, (CHUNK, pw_), 0)
    lane_p = lax.broadcasted_iota(jnp.int32, (CHUNK, pw_), 1)
    jj = lane_p % CHUNK
    incl = ii >= jj
    strict = ii > jj
    eye = (ii == jj).astype(F32)
    head_p = (lane_p // CHUNK).astype(F32).astype(BF16)
    head_w = (lax.broadcasted_iota(jnp.int32, (CHUNK, MIX_W), 1) // dh).astype(F32).astype(BF16)
    sel_p = [head_p == h for h in range(nh)]
    sel_w = [head_w == h for h in range(nh)]
    n_chunks = tile // CHUNK

    def block_diag(y, sel):
        return jnp.concatenate([jnp.where(sel[h], y, jnp.zeros_like(y)) for h in range(nh)], axis=0)

    def head_cols(x, lo, width):
        return jnp.concatenate([jnp.broadcast_to(x[:, lo + h:lo + h + 1], (x.shape[0], width)) for h in range(nh)],
                               axis=1)

    gcums = []
    for c in range(n_chunks):
        rows = slice(c * CHUNK, (c + 1) * CHUNK)
        g3 = jnp.concatenate([sm_ref[0, rows, :], sm_ref[1, rows, :], sm_ref[2, rows, :]], axis=0)
        gcums.append(_dot(tril_ref[...], g3))
    prob = []
    for c in range(n_chunks):
        rows = slice(c * CHUNK, (c + 1) * CHUNK)
        gcum = gcums[c]
        gcum_t = gcum.T
        beta_c = jax.nn.sigmoid(p_ref[rows, qkv + MIX_W:qkv + MIX_W + LANE])
        qc, kc, vc = q_ref[rows, :], k_ref[rows, :], v_ref[rows, :]
        gcol_p = head_cols(gcum, nh, CHUNK)
        grow_p = jnp.concatenate([gcum_t[nh + h:nh + h + 1, :] for h in range(nh)], axis=1)
        decay = jnp.exp(jnp.where(incl, gcol_p - grow_p, -jnp.inf))
        gcol_w = head_cols(gcum, nh, dh)
        glast_w = gcol_w[CHUNK - 1:CHUNK, :]
        beta_w = head_cols(beta_c, 0, dh)
        egc = jnp.exp(gcol_w)
        kb = kc * beta_w
        k_bd = block_diag(kc.astype(BF16), sel_w)
        kk_qk = _dot_nt(jnp.concatenate([kb.astype(BF16), qc.astype(BF16)], axis=0), k_bd)
        lower = jnp.where(strict, kk_qk[0:CHUNK] * decay, 0.0)
        prob.append(dict(
            lower=lower, vb_bd=block_diag((vc * beta_w).astype(BF16), sel_w),
            kbg_bd=block_diag((kb * egc).astype(BF16), sel_w),
            qk=(kk_qk[CHUNK:2 * CHUNK] * decay).astype(BF16), q_dec=(qc * egc).astype(BF16),
            k_dec=(kc * jnp.exp(glast_w - gcol_w)).astype(BF16), a=jnp.exp(glast_w)))

    def rhs3_bd(hi, lo):
        bh, bl = block_diag(hi, sel_p), block_diag(lo, sel_p)
        return jnp.concatenate([bh, bl, bh], axis=0)

    t_inv = [eye - pr["lower"] for pr in prob]
    pw = []
    for pr in prob:
        hi, lo = _split2(pr["lower"])
        pw.append(_mm3(_lhs3(hi, lo), rhs3_bd(hi, lo)))
    for it in range(5):
        for i in range(n_chunks):
            hi, lo = _split2(pw[i])
            lhs_t = _lhs3(*_split2(t_inv[i]))
            if it < 4:
                both = _mm3(jnp.concatenate([lhs_t, _lhs3(hi, lo)], axis=0), rhs3_bd(hi, lo))
                t_inv[i] = t_inv[i] + both[0:CHUNK]
                pw[i] = both[CHUNK:2 * CHUNK]
            else:
                t_inv[i] = t_inv[i] + _mm3(lhs_t, rhs3_bd(hi, lo))
    for pr, t in zip(prob, t_inv):
        tb = t.astype(BF16)
        pr["u"] = _dot(tb, pr["vb_bd"])
        pr["w"] = _dot(tb, pr["kbg_bd"]).astype(BF16)

    st = [st_ref[h] for h in range(nh)]
    for c in range(n_chunks):
        rows = slice(c * CHUNK, (c + 1) * CHUNK)
        pr = prob[c]
        hcols = [slice(h * dh, (h + 1) * dh) for h in range(nh)]
        sb = [st[h].astype(BF16) for h in range(nh)]
        ws = [_dot(pr["w"][:, hcols[h]], sb[h]) for h in range(nh)]
        qs = [_dot(pr["q_dec"][:, hcols[h]], sb[h]) for h in range(nh)]
        vnb = [(pr["u"][:, hcols[h]] - ws[h]).astype(BF16) for h in range(nh)]
        for h in range(nh):
            o_ref[rows, hcols[h]] = qs[h] + _dot(pr["qk"][:, h * CHUNK:(h + 1) * CHUNK], vnb[h])
        for h in range(nh):
            st[h] = st[h] * pr["a"][:, hcols[h]] + _dot_tn(pr["k_dec"][:, hcols[h]], vnb[h])
    for h in range(nh):
        st_ref[h] = st[h]

    g_off = qkv
    for h in range(nh):
        cols = slice(h * dh, (h + 1) * dh)
        y = _head_norm_gate(o_ref[:, cols], hw_ref[...], p_ref[:, g_off + h * dh:g_off + (h + 1) * dh])
        y_ref[:, cols] = y.astype(y_ref.dtype)


def _delta_call(h, norm_w, w, conv_w, alog_row, dtb_row, head_w, tril):
    bsz, seq, d = h.shape
    tile = min(MIX_TILE, seq)
    ncols = w.shape[1]
    full = lambda a: pl.BlockSpec(a.shape, lambda b, t: (0,) * a.ndim)
    return pl.pallas_call(
        _delta_kernel,
        out_shape=jax.ShapeDtypeStruct((bsz, seq, MIX_W), BF16),
        grid=(bsz, seq // tile),
        in_specs=[pl.BlockSpec((None, tile, d), lambda b, t: (b, t, 0)),
                  full(norm_w), full(w), full(conv_w), full(alog_row), full(dtb_row), full(head_w), full(tril)],
        out_specs=pl.BlockSpec((None, tile, MIX_W), lambda b, t: (b, t, 0)),
        scratch_shapes=[pltpu.VMEM((tile, ncols), F32), pltpu.VMEM((tile + SUBLANE, 3 * MIX_W), F32),
                        pltpu.VMEM((tile, MIX_W), F32), pltpu.VMEM((tile, MIX_W), F32),
                        pltpu.VMEM((tile, MIX_W), F32), pltpu.VMEM((3, tile, LANE), BF16),
                        pltpu.VMEM((tile, MIX_W), F32), pltpu.VMEM((DN_HEADS, DN_DH, DN_DH), F32)],
        compiler_params=_cparams(2),
        name="mix_delta",
    )(h, norm_w, w, conv_w, alog_row, dtb_row, head_w, tril)


def _s5_kernel(h_ref, nw_ref, w_ref, bre_ref, bim_ref, ar_ref, ai_ref, cre_ref, cim_ref, d_ref, gw_ref, gb_ref,
               y_ref, u_ref, xr_ref, xi_ref, sr_ref, si_ref):
    bsz, steps, d = h_ref.shape
    rows = bsz * steps
    n_cl = bre_ref.shape[0]
    cw = bre_ref.shape[1]
    sw = bre_ref.shape[2]

    @pl.when(pl.program_id(0) == 0)
    def _():
        sr_ref[...] = jnp.zeros_like(sr_ref)
        si_ref[...] = jnp.zeros_like(si_ref)

    xn = _rms(h_ref[...], nw_ref[...]).astype(BF16).reshape(rows, d)
    p = _dot(xn, w_ref[...])
    n_lt = u_ref.shape[0]
    for b in range(bsz):
        for j in range(n_lt):
            u_ref[j, pl.ds(b, steps, stride=bsz), :] = p[b * steps:(b + 1) * steps, j * LANE:(j + 1) * LANE]

    u = jnp.concatenate([u_ref[j] for j in range(n_lt)], axis=1)
    ub = u.astype(BF16)
    def input_term(c):
        uc = ub[:, c * cw:(c + 1) * cw]
        xr_ref[:, c * sw:(c + 1) * sw] = _dot(uc, bre_ref[c])
        xi_ref[:, c * sw:(c + 1) * sw] = _dot(uc, bim_ref[c])

    def scan(c):
        ls = slice(c * sw, (c + 1) * sw)
        ar, ai = ar_ref[:, ls], ai_ref[:, ls]
        xr, xi = sr_ref[:, ls], si_ref[:, ls]
        for i in range(steps):
            rr = slice(i * bsz, (i + 1) * bsz)
            xr, xi = ar * xr - ai * xi + xr_ref[rr, ls], ar * xi + ai * xr + xi_ref[rr, ls]
            xr_ref[rr, ls] = xr
            xi_ref[rr, ls] = xi
        sr_ref[:, ls] = xr
        si_ref[:, ls] = xi

    def output_term(c):
        ls = slice(c * sw, (c + 1) * sw)
        return _dot(xr_ref[:, ls].astype(BF16), cre_ref[c]) - _dot(xi_ref[:, ls].astype(BF16), cim_ref[c])

    ys = [None] * n_cl
    input_term(0)
    for c in range(n_cl):
        if c + 1 < n_cl:
            input_term(c + 1)
        if c > 0:
            ys[c - 1] = output_term(c - 1)
        scan(c)
    ys[n_cl - 1] = output_term(n_cl - 1)
    y = jnp.concatenate(ys, axis=1) + d_ref[...] * u
    z = jax.nn.gelu(y)
    gl = _dot(z.astype(BF16), gw_ref[...]) + gb_ref[...]
    out = z * jax.nn.sigmoid(gl)
    for j in range(n_lt):
        u_ref[j] = out[:, j * LANE:(j + 1) * LANE]
    for b in range(bsz):
        for j in range(n_lt):
            y_ref[b, :, j * LANE:(j + 1) * LANE] = u_ref[j, pl.ds(b, steps, stride=bsz), :].astype(y_ref.dtype)


def _s5_call(h, norm_w, w, bre, bim, ar, ai, cre, cim, d_row, glu_w, glu_b):
    bsz, seq, d = h.shape
    steps = min(S5_STEPS, seq)
    rows = steps * bsz
    n_state = ar.shape[1]
    full = lambda a: pl.BlockSpec(a.shape, lambda t: (0,) * a.ndim)
    return pl.pallas_call(
        _s5_kernel,
        out_shape=jax.ShapeDtypeStruct((bsz, seq, MIX_W), BF16),
        grid=(seq // steps,),
        in_specs=[pl.BlockSpec((bsz, steps, d), lambda t: (0, t, 0)), full(norm_w), full(w),
                  full(bre), full(bim), full(ar), full(ai), full(cre), full(cim), full(d_row), full(glu_w),
                  full(glu_b)],
        out_specs=pl.BlockSpec((bsz, steps, MIX_W), lambda t: (0, t, 0)),
        scratch_shapes=[pltpu.VMEM((MIX_W // LANE, rows, LANE), F32),
                        pltpu.VMEM((rows, n_state), F32), pltpu.VMEM((rows, n_state), F32),
                        pltpu.VMEM((bsz, n_state), F32), pltpu.VMEM((bsz, n_state), F32)],
        compiler_params=_cparams(1),
        name="mix_s5",
    )(h, norm_w, w, bre, bim, ar, ai, cre, cim, d_row, glu_w, glu_b)


def _s5_params(a_re, a_im, log_dt, b_re, b_im, c_re, c_im, bsz):
    dt = jnp.exp(log_dt)[:, None]
    mag = jnp.exp(a_re * dt)
    abar_r, abar_i = mag * jnp.cos(a_im * dt), mag * jnp.sin(a_im * dt)
    den = a_re * a_re + a_im * a_im
    nr, ni = abar_r - 1.0, abar_i
    zr, zi = (nr * a_re + ni * a_im) / den, (ni * a_re - nr * a_im) / den
    bbar_r = zr[..., None] * b_re - zi[..., None] * b_im
    bbar_i = zr[..., None] * b_im + zi[..., None] * b_re
    n_cl = S5_GROUPS // S5_CLUSTER
    eye = jnp.eye(S5_CLUSTER, dtype=F32)

    def pack_b(bb):
        bb = bb.reshape(n_cl, S5_CLUSTER, S5_STATE, S5_GROUP)
        out = jnp.einsum("cgph,gk->cghkp", bb, eye)
        return out.reshape(n_cl, S5_CLUSTER * S5_GROUP, S5_CLUSTER * S5_STATE).astype(BF16)

    def pack_c(cc):
        cc = cc.reshape(n_cl, S5_CLUSTER, S5_GROUP, S5_STATE)
        out = jnp.einsum("cghp,gk->cgpkh", cc, eye)
        return out.reshape(n_cl, S5_CLUSTER * S5_STATE, S5_CLUSTER * S5_GROUP).astype(BF16)

    bc = lambda a: jnp.broadcast_to(a.reshape(1, -1), (bsz, S5_GROUPS * S5_STATE))
    return pack_b(bbar_r), pack_b(bbar_i), bc(abar_r), bc(abar_i), pack_c(c_re), pack_c(c_im)


def _merge_kernel(h_ref, nw_ref, wm_ref, ya_ref, yb_ref, yc_ref, yd_ref, wb_ref, wo_ref, o_ref):
    h = h_ref[...]
    xn = _rms(h, nw_ref[...]).astype(BF16)
    d = h.shape[1]
    mixed = jnp.zeros(h.shape, F32)
    for n, y_ref in enumerate((ya_ref, yb_ref, yc_ref, yd_ref)):
        gate = jax.nn.sigmoid(_dot(xn, wm_ref[:, n * d:(n + 1) * d]))
        mixed = mixed + gate * _dot(y_ref[...], wb_ref[n])
    o_ref[...] = h + _dot(mixed.astype(BF16), wo_ref[...])


def _merge_call(h, norm_w, w_m, y_a, y_b, y_c, y_d, w_b, w_o):
    bsz, seq, d = h.shape
    tile = min(ROW_TILE, seq)
    full = lambda a: pl.BlockSpec(a.shape, lambda b, t: (0,) * a.ndim)
    ytile = pl.BlockSpec((None, tile, MIX_W), lambda b, t: (b, t, 0))
    return pl.pallas_call(
        _merge_kernel,
        out_shape=jax.ShapeDtypeStruct(h.shape, F32),
        grid=(bsz, seq // tile),
        in_specs=[pl.BlockSpec((None, tile, d), lambda b, t: (b, t, 0)), full(norm_w), full(w_m),
                  ytile, ytile, ytile, ytile, full(w_b), full(w_o)],
        out_specs=pl.BlockSpec((None, tile, d), lambda b, t: (b, t, 0)),
        compiler_params=_cparams(2),
        name="merge",
    )(h, norm_w, w_m, y_a, y_b, y_c, y_d, w_b, w_o)


def _ffn_kernel(f_tiles, h_ref, nw_ref, wg_ref, wu_ref, wd_ref, o_ref):
    h = h_ref[...]
    hn = _rms(h, nw_ref[...]).astype(BF16)
    tf = wg_ref.shape[1] // f_tiles
    out = h
    for j in range(f_tiles):
        cols = slice(j * tf, (j + 1) * tf)
        act = (jax.nn.silu(_dot(hn, wg_ref[:, cols])) * _dot(hn, wu_ref[:, cols])).astype(BF16)
        out = out + _dot(act, wd_ref[cols, :])
    o_ref[...] = out


def _ffn_call(h2, norm_w, w_gate, w_up, w_down, f_tiles):
    n_tok, d = h2.shape
    tile = min(ROW_TILE, n_tok)
    resident = lambda a: pl.BlockSpec(a.shape, lambda i: (0, 0), pipeline_mode=pl.Buffered(1))
    return pl.pallas_call(
        functools.partial(_ffn_kernel, f_tiles),
        out_shape=jax.ShapeDtypeStruct(h2.shape, F32),
        grid=(n_tok // tile,),
        in_specs=[pl.BlockSpec((tile, d), lambda i: (i, 0)),
                  pl.BlockSpec(norm_w.shape, lambda i: (0, 0)),
                  resident(w_gate), resident(w_up), resident(w_down)],
        out_specs=pl.BlockSpec((tile, d), lambda i: (i, 0)),
        compiler_params=_cparams(1),
        name="ffn_dense",
    )(h2, norm_w, w_gate, w_up, w_down)


def _route_kernel(h_ref, nw_ref, rt_ref, hn_ref, idx_ref, gate_ref):
    hn = _rms(h_ref[...], nw_ref[...])
    hn_ref[...] = hn.astype(BF16)
    logits = jnp.dot(hn, rt_ref[...], preferred_element_type=F32, precision=HIGHEST)
    lg = logits.T[0:N_EXPERTS, :]
    row = lax.broadcasted_iota(jnp.int32, lg.shape, 0)
    m1 = jnp.max(lg, axis=0, keepdims=True)
    i1 = jnp.min(jnp.where(lg == m1, row, N_EXPERTS), axis=0, keepdims=True)
    lg2 = jnp.where(row == i1, -jnp.inf, lg)
    m2 = jnp.max(lg2, axis=0, keepdims=True)
    i2 = jnp.min(jnp.where(lg2 == m2, row, N_EXPERTS), axis=0, keepdims=True)
    e2 = jnp.exp(m2 - m1)
    g1 = 1.0 / (1.0 + e2)
    g2 = e2 / (1.0 + e2)
    idx_ref[...] = jnp.where(row == 0, i1, jnp.where(row == 1, i2, 0))
    gate_ref[...] = jnp.where(row == 0, g1, jnp.where(row == 1, g2, 0.0))


def _route_call(h2, norm_w, router_pad):
    n_tok, d = h2.shape
    tile = min(ROW_TILE, n_tok)
    return pl.pallas_call(
        _route_kernel,
        out_shape=(jax.ShapeDtypeStruct((n_tok, d), BF16),
                   jax.ShapeDtypeStruct((N_EXPERTS, n_tok), jnp.int32),
                   jax.ShapeDtypeStruct((N_EXPERTS, n_tok), F32)),
        grid=(n_tok // tile,),
        in_specs=[pl.BlockSpec((tile, d), lambda i: (i, 0)),
                  pl.BlockSpec(norm_w.shape, lambda i: (0, 0)),
                  pl.BlockSpec(router_pad.shape, lambda i: (0, 0))],
        out_specs=(pl.BlockSpec((tile, d), lambda i: (i, 0)),
                   pl.BlockSpec((N_EXPERTS, tile), lambda i: (0, i)),
                   pl.BlockSpec((N_EXPERTS, tile), lambda i: (0, i))),
        compiler_params=_cparams(1),
        name="moe_route",
    )(h2, norm_w, router_pad)


def _expert_kernel(be_ref, bv_ref, x_ref, wg_ref, wu_ref, wd_ref, o_ref, acc_ref):
    i, j = pl.program_id(0), pl.program_id(1)
    last = pl.num_programs(1) - 1

    @pl.when(j == 0)
    def _():
        acc_ref[...] = jnp.zeros_like(acc_ref)

    @pl.when(bv_ref[i] != 0)
    def _():
        x = x_ref[...]
        act = (jax.nn.silu(_dot(x, wg_ref[...])) * _dot(x, wu_ref[...])).astype(BF16)
        acc_ref[...] += _dot(act, wd_ref[...])

    @pl.when(j == last)
    def _():
        o_ref[...] = acc_ref[...].astype(o_ref.dtype)


def _expert_call(blk_e, blk_valid, xs, w_gate, w_up, w_down, f_tiles):
    n_rows, d = xs.shape
    f = w_gate.shape[2]
    tf = f // f_tiles
    n_blk = n_rows // MOE_TILE
    grid_spec = pltpu.PrefetchScalarGridSpec(
        num_scalar_prefetch=2,
        grid=(n_blk, f_tiles),
        in_specs=[pl.BlockSpec((MOE_TILE, d), lambda i, j, be, bv: (i, 0)),
                  pl.BlockSpec((None, d, tf), lambda i, j, be, bv: (be[i], 0, j)),
                  pl.BlockSpec((None, d, tf), lambda i, j, be, bv: (be[i], 0, j)),
                  pl.BlockSpec((None, tf, d), lambda i, j, be, bv: (be[i], j, 0))],
        out_specs=pl.BlockSpec((MOE_TILE, d), lambda i, j, be, bv: (i, 0)),
        scratch_shapes=[pltpu.VMEM((MOE_TILE, d), F32)],
    )
    return pl.pallas_call(
        _expert_kernel,
        out_shape=jax.ShapeDtypeStruct((n_rows, d), BF16),
        grid_spec=grid_spec,
        compiler_params=_cparams(2),
        name="moe_experts",
    )(blk_e, blk_valid, xs, w_gate, w_up, w_down)


def _combine_kernel(apply_norm, h_ref, y0_ref, y1_ref, g0_ref, g1_ref, nw_ref, o_ref):
    out = h_ref[...] + (g0_ref[...] * y0_ref[...].astype(F32) + g1_ref[...] * y1_ref[...].astype(F32))
    if apply_norm:
        out = _rms(out, nw_ref[...])
    o_ref[...] = out


def _combine_call(h2, y0, y1, g0, g1, norm_w, apply_norm):
    n_tok, d = h2.shape
    tile = min(ROW_TILE, n_tok)
    row = pl.BlockSpec((tile, d), lambda i: (i, 0))
    col = pl.BlockSpec((tile, 1), lambda i: (i, 0))
    return pl.pallas_call(
        functools.partial(_combine_kernel, apply_norm),
        out_shape=jax.ShapeDtypeStruct(h2.shape, F32),
        grid=(n_tok // tile,),
        in_specs=[row, row, row, col, col, pl.BlockSpec(norm_w.shape, lambda i: (0, 0))],
        out_specs=row,
        compiler_params=_cparams(1),
        name="moe_combine",
    )(h2, y0, y1, g0, g1, norm_w)


def _final_norm_kernel(h_ref, nw_ref, o_ref):
    o_ref[...] = _rms(h_ref[...], nw_ref[...])


def _final_norm_call(h2, norm_w):
    n_tok, d = h2.shape
    tile = min(ROW_TILE, n_tok)
    row = pl.BlockSpec((tile, d), lambda i: (i, 0))
    return pl.pallas_call(
        _final_norm_kernel,
        out_shape=jax.ShapeDtypeStruct(h2.shape, F32),
        grid=(n_tok // tile,),
        in_specs=[row, pl.BlockSpec(norm_w.shape, lambda i: (0, 0))],
        out_specs=row,
        compiler_params=_cparams(1),
        name="final_norm",
    )(h2, norm_w)


def _moe(h2, norm_w, router, w_gate, w_up, w_down, final_w, apply_final):
    n_tok, d = h2.shape
    router_pad = jnp.zeros((d, LANE), F32).at[:, :N_EXPERTS].set(router)
    hn, idx, gate = _route_call(h2, norm_w, router_pad)
    e_flat = idx[:TOP_K].reshape(-1)
    tok_flat = jnp.tile(jnp.arange(n_tok, dtype=jnp.int32), TOP_K)
    onehot = (e_flat[:, None] == jnp.arange(N_EXPERTS, dtype=jnp.int32)[None, :]).astype(jnp.int32)
    csum = jnp.cumsum(onehot, axis=0)
    rank = jnp.sum(csum * onehot, axis=1) - 1
    counts = csum[-1]
    padded = (counts + MOE_TILE - 1) // MOE_TILE * MOE_TILE
    pad_end = jnp.cumsum(padded)
    pad_start = pad_end - padded
    dest = pad_start[e_flat] + rank
    n_assign = n_tok * TOP_K
    n_blk = -(-n_assign // MOE_TILE) + N_EXPERTS
    src_tok = jnp.zeros((n_blk * MOE_TILE,), jnp.int32).at[dest].set(
        tok_flat, mode="promise_in_bounds", unique_indices=True)
    blk_start = jnp.arange(n_blk, dtype=jnp.int32) * MOE_TILE
    blk_e = jnp.minimum(jnp.searchsorted(pad_end, blk_start, side="right"), N_EXPERTS - 1).astype(jnp.int32)
    blk_valid = (blk_start < pad_end[-1]).astype(jnp.int32)
    xs = hn.at[src_tok].get(mode="promise_in_bounds")
    f_tiles = 2 if w_gate.shape[2] % (2 * LANE) == 0 else 1
    ys = _expert_call(blk_e, blk_valid, xs, w_gate, w_up, w_down, f_tiles)
    y0 = ys.at[dest[:n_tok]].get(mode="promise_in_bounds")
    y1 = ys.at[dest[n_tok:]].get(mode="promise_in_bounds")
    g0 = gate[0].reshape(n_tok, 1)
    g1 = gate[1].reshape(n_tok, 1)
    return _combine_call(h2, y0, y1, g0, g1, final_w, apply_final)


def _pad_cols(w, n):
    return jnp.concatenate([w, jnp.zeros((w.shape[0], n - w.shape[1]), w.dtype)], axis=1)


def _token_mix(h, norm_w, w_in, lb, hgrn_norm, s5, s5_d, s5_glu_w, s5_glu_b, dn_conv_w, dn_a_log, dn_dt_bias,
               dn_norm, gla_w_lr, gla_b_lr, gla_norm, w_branch, w_out, rs, tril):
    bsz, seq, d = h.shape
    m = MIX_W
    o = 0
    w_a = w_in[:, o:o + 4 * m]; o += 4 * m
    w_s5 = w_in[:, o:o + m]; o += m
    w_qkv = w_in[:, o:o + 3 * m]; o += 3 * m
    w_bd = w_in[:, o:o + 2 * DN_HEADS]; o += 2 * DN_HEADS
    w_dg = w_in[:, o:o + m]; o += m
    nqk = GLA_HEADS * GLA_DK
    w_gqk = w_in[:, o:o + 2 * nqk]; o += 2 * nqk
    w_gv = w_in[:, o:o + m]; o += m
    w_glr = w_in[:, o:o + GLA_RANK]; o += GLA_RANK
    w_gg = w_in[:, o:o + m]; o += m
    w_m = w_in[:, o:o + N_BRANCH * d]

    w_a = w_a.astype(BF16)
    w_c = jnp.concatenate([w_qkv, w_dg, _pad_cols(w_bd, LANE)], axis=1).astype(BF16)
    w_d = jnp.concatenate([w_gqk, w_gv, w_gg, _pad_cols(w_glr, LANE)], axis=1).astype(BF16)
    row = lambda a: a.reshape(1, -1).astype(F32)
    nw = row(norm_w)

    dummy = jnp.zeros((1, LANE), F32)
    y_a = _gla_call("hgrn", h, nw, w_a, row(lb), dummy, row(hgrn_norm), rs)

    wlr_pad = jnp.zeros((LANE, nqk), F32).at[:GLA_RANK].set(gla_w_lr)
    y_d = _gla_call("gla", h, nw, w_d, wlr_pad, row(gla_b_lr), row(gla_norm), rs)

    lane_row = lambda a: jnp.zeros((1, LANE), F32).at[0, DN_HEADS:2 * DN_HEADS].set(a)
    y_c = _delta_call(h, nw, w_c, dn_conv_w.astype(F32), lane_row(dn_a_log), lane_row(dn_dt_bias), row(dn_norm), tril)

    bre, bim, ar, ai, cre, cim = s5
    y_b = _s5_call(h, nw, w_s5.astype(BF16), bre, bim, ar, ai, cre, cim, row(s5_d), s5_glu_w.astype(BF16),
                   row(s5_glu_b))

    return _merge_call(h, nw, w_m.astype(BF16), y_a, y_b, y_c, y_d, w_branch.astype(BF16), w_out.astype(BF16))


def kernel(x, attn_norm, ffn_norm, final_norm, w_in, hgrn_lb, hgrn_norm, s5_a_re, s5_a_im, s5_log_dt, s5_b_re, s5_b_im, s5_c_re, s5_c_im, s5_d, s5_glu_w, s5_glu_b, dn_conv_w, dn_a_log, dn_dt_bias, dn_norm, gla_w_lr, gla_b_lr, gla_norm, w_branch, w_out, ffn_w_gate, ffn_w_up, ffn_w_down, moe_router, moe_w_gate, moe_w_up, moe_w_down):
    bsz, seq, d = x.shape
    depth = w_in.shape[0]
    assert bsz == SUBLANE, "the S5 kernel keeps one batch row per sublane"
    assert seq % CHUNK == 0 and d == D_MODEL
    lb_soft = jax.nn.softmax(hgrn_lb.astype(F32), axis=0)
    lb_all = jnp.cumsum(lb_soft, axis=0) - lb_soft[0]
    rs = jnp.asarray(_range_sum_matrix(), BF16)
    tril = jnp.asarray(np.tile(np.tril(np.ones((CHUNK, CHUNK), np.float32)), (1, 3)), BF16)
    row = lambda a: a.reshape(1, -1).astype(F32)

    h = x
    for layer in range(depth):
        s5 = _s5_params(s5_a_re[layer], s5_a_im[layer], s5_log_dt[layer], s5_b_re[layer], s5_b_im[layer],
                        s5_c_re[layer], s5_c_im[layer], bsz)
        h = _token_mix(h, attn_norm[layer], w_in[layer], lb_all[layer], hgrn_norm[layer], s5, s5_d[layer],
                       s5_glu_w[layer], s5_glu_b[layer], dn_conv_w[layer], dn_a_log[layer], dn_dt_bias[layer],
                       dn_norm[layer], gla_w_lr[layer], gla_b_lr[layer], gla_norm[layer], w_branch[layer],
                       w_out[layer], rs, tril)
        h2 = h.reshape(bsz * seq, d)
        j = layer // 2
        last = layer == depth - 1
        if layer % 2 == 0:
            f = ffn_w_gate.shape[2]
            f_tiles = 2 if f % (2 * LANE) == 0 else 1
            h2 = _ffn_call(h2, row(ffn_norm[layer]), ffn_w_gate[j].astype(BF16), ffn_w_up[j].astype(BF16),
                           ffn_w_down[j].astype(BF16), f_tiles)
            if last:
                h2 = _final_norm_call(h2, row(final_norm))
        else:
            h2 = _moe(h2, row(ffn_norm[layer]), moe_router[j], moe_w_gate[j].astype(BF16),
                      moe_w_up[j].astype(BF16), moe_w_down[j].astype(BF16), row(final_norm), last)
        h = h2.reshape(bsz, seq, d)
    return h
```

```python
import functools
import math

import numpy as np
import jax
import jax.numpy as jnp
from jax import lax
from jax.experimental import pallas as pl
from jax.experimental.pallas import tpu as pltpu

F32 = jnp.float32
BF16 = jnp.bfloat16
HIGHEST = lax.Precision.HIGHEST

D_MODEL = 1024
MIX_W = 512
HG_HEADS = 4
HG_DH = MIX_W // HG_HEADS
S5_GROUP = 16
S5_GROUPS = MIX_W // S5_GROUP
S5_STATE = 64
DN_HEADS = 4
DN_DH = MIX_W // DN_HEADS
CONV_K = 4
GLA_HEADS = 4
GLA_DK = 64
GLA_DV = MIX_W // GLA_HEADS
GLA_RANK = 16
GLA_NORMALIZER = 16.0
CHUNK = 64
N_BRANCH = 4
N_EXPERTS = 8
TOP_K = 2
NORM_EPS = 1e-6

LANE = 128
SUBLANE = 8
VMEM_LIMIT = 56 * 1024 * 1024

MIX_TILE = 512
S5_STEPS = 128
S5_CLUSTER = 8
ROW_TILE = 512
MOE_TILE = 512
LEVELS = (32, 16, 8, 4, 2, 1)


def _cparams(n_axes):
    return pltpu.CompilerParams(dimension_semantics=("arbitrary",) * n_axes, vmem_limit_bytes=VMEM_LIMIT)


def _rms(x, w):
    return x * lax.rsqrt(jnp.mean(x * x, axis=-1, keepdims=True) + NORM_EPS) * w


def _dot(a, b):
    return jnp.dot(a, b, preferred_element_type=F32)


def _dot_nt(a, b, precision=None):
    return lax.dot_general(a, b, (((1,), (1,)), ((), ())), preferred_element_type=F32, precision=precision)


def _dot_tn(a, b):
    return lax.dot_general(a, b, (((0,), (0,)), ((), ())), preferred_element_type=F32)


def _split3(x):
    hi = x.astype(BF16)
    r1 = x - hi.astype(F32)
    mid = r1.astype(BF16)
    lo = (r1 - mid.astype(F32)).astype(BF16)
    return hi, mid, lo


def _split2(x):
    hi = x.astype(BF16)
    return hi, (x - hi.astype(F32)).astype(BF16)


def _lhs3(hi, lo):
    return jnp.concatenate([hi, hi, lo], axis=1)


def _rhs3(hi, lo):
    return jnp.concatenate([hi, lo, hi], axis=0)


def _mm3(lhs3, rhs3):
    return jnp.dot(lhs3, rhs3, preferred_element_type=F32)


def _range_sum_matrix():
    c = CHUNK
    m = np.zeros((2 + len(LEVELS), c, c), np.float32)
    for r in range(c):
        m[0, r, :r + 1] = 1.0
        m[1, r, r + 1:] = 1.0
        for l, b in enumerate(LEVELS):
            mid = (r // (2 * b)) * 2 * b + b - 1
            if r > mid:
                m[2 + l, r, mid + 1:r + 1] = 1.0
            else:
                m[2 + l, r, r + 1:mid + 1] = 1.0
    m = m.reshape(-1, c)
    return np.concatenate([m, m, m], axis=1)


def _head_norm_gate(o, w, gate):
    return o * lax.rsqrt(jnp.mean(o * o, axis=-1, keepdims=True) + NORM_EPS) * w * jax.nn.silu(gate)


def _gla_chunks(n_heads, dk, dv, q_ref, k_ref, v_ref, gs_ref, rs_ref, o_ref, st_ref, n_chunks):
    ii = lax.broadcasted_iota(jnp.int32, (CHUNK, CHUNK), 0)
    jj = lax.broadcasted_iota(jnp.int32, (CHUNK, CHUNK), 1)
    masks = [((ii // (2 * b)) == (jj // (2 * b))) & ((ii % (2 * b)) >= b) & ((jj % (2 * b)) < b) for b in LEVELS]
    chunks = range(n_chunks)
    heads = range(n_heads)
    rows = [slice(c * CHUNK, (c + 1) * CHUNK) for c in chunks]

    xs = []
    for c in chunks:
        g3 = jnp.concatenate([gs_ref[0, rows[c], :], gs_ref[1, rows[c], :], gs_ref[2, rows[c], :]], axis=0)
        xs.append(jnp.exp(_dot(rs_ref[...], g3)))

    scores, kv, qds, decs, diags = {}, {}, {}, {}, {}
    for c in chunks:
        qc, kc, vc = q_ref[rows[c], :], k_ref[rows[c], :], v_ref[rows[c], :]
        x = xs[c]
        for h in heads:
            cols = slice(h * dk, (h + 1) * dk)
            qh, kh = qc[:, cols], kc[:, cols]
            vh = vc[:, h * dv:(h + 1) * dv]
            a = jnp.zeros((CHUNK, CHUNK), F32)
            for l in range(len(LEVELS)):
                xl = x[(2 + l) * CHUNK:(3 + l) * CHUNK, cols]
                al = _dot_nt((qh * xl).astype(BF16), (kh * xl).astype(BF16))
                a = jnp.where(masks[l], al, a)
            scores[c, h] = a.astype(BF16)
            kv[c, h] = _dot_tn(vh.astype(BF16), (kh * x[CHUNK:2 * CHUNK, cols]).astype(BF16))
            qds[c, h] = (qh * x[0:CHUNK, cols]).astype(BF16)
            decs[c, h] = x[CHUNK - 1:CHUNK, cols]
            diags[c, h] = jnp.sum(qh * kh, axis=-1, keepdims=True) * vh

    for c in chunks:
        vc = v_ref[rows[c], :]
        for h in heads:
            vb = vc[:, h * dv:(h + 1) * dv].astype(BF16)
            o_ref[rows[c], h * dv:(h + 1) * dv] = _dot(scores[c, h], vb) + diags[c, h]

    for h in heads:
        st = st_ref[h]
        for c in chunks:
            o_ref[rows[c], h * dv:(h + 1) * dv] += _dot_nt(qds[c, h], st.astype(BF16))
            st = st * decs[c, h] + kv[c, h]
        st_ref[h] = st


def _gla_kernel(mode, h_ref, nw_ref, w_ref, aux1_ref, aux2_ref, hw_ref, rs_ref, y_ref,
                p_ref, q_ref, k_ref, gs_ref, o_ref, st_ref):
    if mode == "hgrn":
        n_heads, dk, dv = HG_HEADS, HG_DH, HG_DH
    else:
        n_heads, dk, dv = GLA_HEADS, GLA_DK, GLA_DV
    n = n_heads * dk
    tile = h_ref.shape[0]

    @pl.when(pl.program_id(1) == 0)
    def _():
        st_ref[...] = jnp.zeros_like(st_ref)

    xn = _rms(h_ref[...], nw_ref[...]).astype(BF16)
    p_ref[...] = _dot(xn, w_ref[...])
    if mode == "hgrn":
        lb = aux1_ref[...]
        z = p_ref[:, n:2 * n]
        g = jnp.log(lb + (1.0 - lb) * jax.nn.sigmoid(z))
        q_ref[...] = p_ref[:, 0:n] * dk ** -0.5
        k_ref[...] = (1.0 - lb) * jax.nn.sigmoid(-z)
        v_off = 2 * n
    else:
        v_off = 2 * n
        glr = p_ref[:, v_off + 2 * MIX_W:v_off + 2 * MIX_W + LANE]
        logit = jnp.dot(glr, aux1_ref[...], preferred_element_type=F32, precision=HIGHEST) + aux2_ref[...]
        g = jax.nn.log_sigmoid(logit) / GLA_NORMALIZER
        q_ref[...] = p_ref[:, 0:n] * dk ** -0.5
        k_ref[...] = p_ref[:, n:2 * n]
    hi, mid, lo = _split3(g)
    gs_ref[0] = hi
    gs_ref[1] = mid
    gs_ref[2] = lo

    _gla_chunks(n_heads, dk, dv, q_ref, k_ref, p_ref.at[:, v_off:v_off + MIX_W], gs_ref, rs_ref, o_ref, st_ref,
                tile // CHUNK)

    g_off = v_off + MIX_W
    for h in range(n_heads):
        cols = slice(h * dv, (h + 1) * dv)
        y = _head_norm_gate(o_ref[:, cols], hw_ref[...], p_ref[:, g_off + h * dv:g_off + (h + 1) * dv])
        y_ref[:, cols] = y.astype(y_ref.dtype)


def _gla_call(mode, h, norm_w, w, aux1, aux2, head_w, rs):
    bsz, seq, d = h.shape
    n_heads, dk, dv = (HG_HEADS, HG_DH, HG_DH) if mode == "hgrn" else (GLA_HEADS, GLA_DK, GLA_DV)
    n = n_heads * dk
    tile = min(MIX_TILE, seq)
    ncols = w.shape[1]
    full = lambda a: pl.BlockSpec(a.shape, lambda b, t: (0,) * a.ndim)
    return pl.pallas_call(
        functools.partial(_gla_kernel, mode),
        out_shape=jax.ShapeDtypeStruct((bsz, seq, MIX_W), BF16),
        grid=(bsz, seq // tile),
        in_specs=[pl.BlockSpec((None, tile, d), lambda b, t: (b, t, 0)),
                  full(norm_w), full(w), full(aux1), full(aux2), full(head_w), full(rs)],
        out_specs=pl.BlockSpec((None, tile, MIX_W), lambda b, t: (b, t, 0)),
        scratch_shapes=[pltpu.VMEM((tile, ncols), F32), pltpu.VMEM((tile, n), F32), pltpu.VMEM((tile, n), F32),
                        pltpu.VMEM((3, tile, n), BF16), pltpu.VMEM((tile, MIX_W), F32),
                        pltpu.VMEM((n_heads, dv, dk), F32)],
        compiler_params=_cparams(2),
        name="mix_" + mode,
    )(h, norm_w, w, aux1, aux2, head_w, rs)


def _delta_kernel(h_ref, nw_ref, w_ref, cw_ref, alog_ref, dtb_ref, hw_ref, tril_ref, y_ref,
                  p_ref, xs_ref, q_ref, k_ref, v_ref, sm_ref, o_ref, st_ref):
    tile = h_ref.shape[0]
    nh, dh = DN_HEADS, DN_DH
    qkv = 3 * MIX_W

    @pl.when(pl.program_id(1) == 0)
    def _():
        st_ref[...] = jnp.zeros_like(st_ref)
        xs_ref[0:SUBLANE, :] = jnp.zeros((SUBLANE, qkv), F32)

    xn = _rms(h_ref[...], nw_ref[...]).astype(BF16)
    p_ref[...] = _dot(xn, w_ref[...])

    xs_ref[SUBLANE:SUBLANE + tile, :] = p_ref[:, 0:qkv]
    conv = cw_ref[CONV_K - 1:CONV_K, :] * xs_ref[SUBLANE:SUBLANE + tile, :]
    for j in range(1, CONV_K):
        conv = conv + cw_ref[CONV_K - 1 - j:CONV_K - j, :] * xs_ref[SUBLANE - j:SUBLANE - j + tile, :]
    xs_ref[0:SUBLANE, :] = xs_ref[tile:tile + SUBLANE, :]
    conv = jax.nn.silu(conv)
    for h in range(nh):
        cols = slice(h * dh, (h + 1) * dh)
        qh = conv[:, h * dh:(h + 1) * dh]
        kh = conv[:, MIX_W + h * dh:MIX_W + (h + 1) * dh]
        q_ref[:, cols] = qh * lax.rsqrt(jnp.sum(qh * qh, axis=-1, keepdims=True) + NORM_EPS) * dh ** -0.5
        k_ref[:, cols] = kh * lax.rsqrt(jnp.sum(kh * kh, axis=-1, keepdims=True) + NORM_EPS)
    v_ref[...] = conv[:, 2 * MIX_W:3 * MIX_W]

    small = p_ref[:, qkv + MIX_W:qkv + MIX_W + LANE]
    gdec = -jnp.exp(alog_ref[...]) * jax.nn.softplus(small + dtb_ref[...])
    hi, mid, lo = _split3(gdec)
    sm_ref[0] = hi
    sm_ref[1] = mid
    sm_ref[2] = lo

    pw_ = nh * CHUNK
    ii = lax.broadcasted_iota(jnp.int32, (CHUNK, pw_), 0)
    lane_p = lax.broadcasted_iota(jnp.int32, (CHUNK, pw_), 1)
    jj = lane_p % CHUNK
    incl = ii >= jj
    strict = ii > jj
    eye = (ii == jj).astype(F32)
    head_p = (lane_p // CHUNK).astype(F32).astype(BF16)
    head_w = (lax.broadcasted_iota(jnp.int32, (CHUNK, MIX_W), 1) // dh).astype(F32).astype(BF16)
    sel_p = [head_p == h for h in range(nh)]
    sel_w = [head_w == h for h in range(nh)]
    n_chunks = tile // CHUNK

    def block_diag(y, sel):
        return jnp.concatenate([jnp.where(sel[h], y, jnp.zeros_like(y)) for h in range(nh)], axis=0)

    def head_cols(x, lo, width):
        return jnp.concatenate([jnp.broadcast_to(x[:, lo + h:lo + h + 1], (x.shape[0], width)) for h in range(nh)],
                               axis=1)

    gcums = []
    for c in range(n_chunks):
        rows = slice(c * CHUNK, (c + 1) * CHUNK)
        g3 = jnp.concatenate([sm_ref[0, rows, :], sm_ref[1, rows, :], sm_ref[2, rows, :]], axis=0)
        gcums.append(_dot(tril_ref[...], g3))
    prob = []
    for c in range(n_chunks):
        rows = slice(c * CHUNK, (c + 1) * CHUNK)
        gcum = gcums[c]
        gcum_t = gcum.T
        beta_c = jax.nn.sigmoid(p_ref[rows, qkv + MIX_W:qkv + MIX_W + LANE])
        qc, kc, vc = q_ref[rows, :], k_ref[rows, :], v_ref[rows, :]
        gcol_p = head_cols(gcum, nh, CHUNK)
        grow_p = jnp.concatenate([gcum_t[nh + h:nh + h + 1, :] for h in range(nh)], axis=1)
        decay = jnp.exp(jnp.where(incl, gcol_p - grow_p, -jnp.inf))
        gcol_w = head_cols(gcum, nh, dh)
        glast_w = gcol_w[CHUNK - 1:CHUNK, :]
        beta_w = head_cols(beta_c, 0, dh)
        egc = jnp.exp(gcol_w)
        kb = kc * beta_w
        k_bd = block_diag(kc.astype(BF16), sel_w)
        kk_qk = _dot_nt(jnp.concatenate([kb.astype(BF16), qc.astype(BF16)], axis=0), k_bd)
        lower = jnp.where(strict, kk_qk[0:CHUNK] * decay, 0.0)
        prob.append(dict(
            lower=lower, vb_bd=block_diag((vc * beta_w).astype(BF16), sel_w),
            kbg_bd=block_diag((kb * egc).astype(BF16), sel_w),
            qk=(kk_qk[CHUNK:2 * CHUNK] * decay).astype(BF16), q_dec=(qc * egc).astype(BF16),
            k_dec=(kc * jnp.exp(glast_w - gcol_w)).astype(BF16), a=jnp.exp(glast_w)))

    def rhs3_bd(hi, lo):
        bh, bl = block_diag(hi, sel_p), block_diag(lo, sel_p)
        return jnp.concatenate([bh, bl, bh], axis=0)

    t_inv = [eye - pr["lower"] for pr in prob]
    pw = []
    for pr in prob:
        hi, lo = _split2(pr["lower"])
        pw.append(_mm3(_lhs3(hi, lo), rhs3_bd(hi, lo)))
    for it in range(5):
        for i in range(n_chunks):
            hi, lo = _split2(pw[i])
            lhs_t = _lhs3(*_split2(t_inv[i]))
            if it < 4:
                both = _mm3(jnp.concatenate([lhs_t, _lhs3(hi, lo)], axis=0), rhs3_bd(hi, lo))
                t_inv[i] = t_inv[i] + both[0:CHUNK]
                pw[i] = both[CHUNK:2 * CHUNK]
            else:
                t_inv[i] = t_inv[i] + _mm3(lhs_t, rhs3_bd(hi, lo))
    for pr, t in zip(prob, t_inv):
        tb = t.astype(BF16)
        pr["u"] = _dot(tb, pr["vb_bd"])
        pr["w"] = _dot(tb, pr["kbg_bd"]).astype(BF16)

    st = [st_ref[h] for h in range(nh)]
    for c in range(n_chunks):
        rows = slice(c * CHUNK, (c + 1) * CHUNK)
        pr = prob[c]
        hcols = [slice(h * dh, (h + 1) * dh) for h in range(nh)]
        sb = [st[h].astype(BF16) for h in range(nh)]
        ws = [_dot(pr["w"][:, hcols[h]], sb[h]) for h in range(nh)]
        qs = [_dot(pr["q_dec"][:, hcols[h]], sb[h]) for h in range(nh)]
        vnb = [(pr["u"][:, hcols[h]] - ws[h]).astype(BF16) for h in range(nh)]
        for h in range(nh):
            o_ref[rows, hcols[h]] = qs[h] + _dot(pr["qk"][:, h * CHUNK:(h + 1) * CHUNK], vnb[h])
        for h in range(nh):
            st[h] = st[h] * pr["a"][:, hcols[h]] + _dot_tn(pr["k_dec"][:, hcols[h]], vnb[h])
    for h in range(nh):
        st_ref[h] = st[h]

    g_off = qkv
    for h in range(nh):
        cols = slice(h * dh, (h + 1) * dh)
        y = _head_norm_gate(o_ref[:, cols], hw_ref[...], p_ref[:, g_off + h * dh:g_off + (h + 1) * dh])
        y_ref[:, cols] = y.astype(y_ref.dtype)


def _delta_call(h, norm_w, w, conv_w, alog_row, dtb_row, head_w, tril):
    bsz, seq, d = h.shape
    tile = min(MIX_TILE, seq)
    ncols = w.shape[1]
    full = lambda a: pl.BlockSpec(a.shape, lambda b, t: (0,) * a.ndim)
    return pl.pallas_call(
        _delta_kernel,
        out_shape=jax.ShapeDtypeStruct((bsz, seq, MIX_W), BF16),
        grid=(bsz, seq // tile),
        in_specs=[pl.BlockSpec((None, tile, d), lambda b, t: (b, t, 0)),
                  full(norm_w), full(w), full(conv_w), full(alog_row), full(dtb_row), full(head_w), full(tril)],
        out_specs=pl.BlockSpec((None, tile, MIX_W), lambda b, t: (b, t, 0)),
        scratch_shapes=[pltpu.VMEM((tile, ncols), F32), pltpu.VMEM((tile + SUBLANE, 3 * MIX_W), F32),
                        pltpu.VMEM((tile, MIX_W), F32), pltpu.VMEM((tile, MIX_W), F32),
                        pltpu.VMEM((tile, MIX_W), F32), pltpu.VMEM((3, tile, LANE), BF16),
                        pltpu.VMEM((tile, MIX_W), F32), pltpu.VMEM((DN_HEADS, DN_DH, DN_DH), F32)],
        compiler_params=_cparams(2),
        name="mix_delta",
    )(h, norm_w, w, conv_w, alog_row, dtb_row, head_w, tril)


def _s5_kernel(h_ref, nw_ref, w_ref, bre_ref, bim_ref, ar_ref, ai_ref, cre_ref, cim_ref, d_ref, gw_ref, gb_ref,
               y_ref, u_ref, xr_ref, xi_ref, sr_ref, si_ref):
    bsz, steps, d = h_ref.shape
    rows = bsz * steps
    n_cl = bre_ref.shape[0]
    cw = bre_ref.shape[1]
    sw = bre_ref.shape[2]

    @pl.when(pl.program_id(0) == 0)
    def _():
        sr_ref[...] = jnp.zeros_like(sr_ref)
        si_ref[...] = jnp.zeros_like(si_ref)

    xn = _rms(h_ref[...], nw_ref[...]).astype(BF16).reshape(rows, d)
    p = _dot(xn, w_ref[...])
    n_lt = u_ref.shape[0]
    for b in range(bsz):
        for j in range(n_lt):
            u_ref[j, pl.ds(b, steps, stride=bsz), :] = p[b * steps:(b + 1) * steps, j * LANE:(j + 1) * LANE]

    u = jnp.concatenate([u_ref[j] for j in range(n_lt)], axis=1)
    ub = u.astype(BF16)
    def input_term(c):
        uc = ub[:, c * cw:(c + 1) * cw]
        xr_ref[:, c * sw:(c + 1) * sw] = _dot(uc, bre_ref[c])
        xi_ref[:, c * sw:(c + 1) * sw] = _dot(uc, bim_ref[c])

    def scan(c):
        ls = slice(c * sw, (c + 1) * sw)
        ar, ai = ar_ref[:, ls], ai_ref[:, ls]
        xr, xi = sr_ref[:, ls], si_ref[:, ls]
        for i in range(steps):
            rr = slice(i * bsz, (i + 1) * bsz)
            xr, xi = ar * xr - ai * xi + xr_ref[rr, ls], ar * xi + ai * xr + xi_ref[rr, ls]
            xr_ref[rr, ls] = xr
            xi_ref[rr, ls] = xi
        sr_ref[:, ls] = xr
        si_ref[:, ls] = xi

    def output_term(c):
        ls = slice(c * sw, (c + 1) * sw)
        return _dot(xr_ref[:, ls].astype(BF16), cre_ref[c]) - _dot(xi_ref[:, ls].astype(BF16), cim_ref[c])

    ys = [None] * n_cl
    input_term(0)
    for c in range(n_cl):
        if c + 1 < n_cl:
            input_term(c + 1)
        if c > 0:
            ys[c - 1] = output_term(c - 1)
        scan(c)
    ys[n_cl - 1] = output_term(n_cl - 1)
    y = jnp.concatenate(ys, axis=1) + d_ref[...] * u
    z = jax.nn.gelu(y)
    gl = _dot(z.astype(BF16), gw_ref[...]) + gb_ref[...]
    out = z * jax.nn.sigmoid(gl)
    for j in range(n_lt):
        u_ref[j] = out[:, j * LANE:(j + 1) * LANE]
    for b in range(bsz):
        for j in range(n_lt):
            y_ref[b, :, j * LANE:(j + 1) * LANE] = u_ref[j, pl.ds(b, steps, stride=bsz), :].astype(y_ref.dtype)


def _s5_call(h, norm_w, w, bre, bim, ar, ai, cre, cim, d_row, glu_w, glu_b):
    bsz, seq, d = h.shape
    steps = min(S5_STEPS, seq)
    rows = steps * bsz
    n_state = ar.shape[1]
    full = lambda a: pl.BlockSpec(a.shape, lambda t: (0,) * a.ndim)
    return pl.pallas_call(
        _s5_kernel,
        out_shape=jax.ShapeDtypeStruct((bsz, seq, MIX_W), BF16),
        grid=(seq // steps,),
        in_specs=[pl.BlockSpec((bsz, steps, d), lambda t: (0, t, 0)), full(norm_w), full(w),
                  full(bre), full(bim), full(ar), full(ai), full(cre), full(cim), full(d_row), full(glu_w),
                  full(glu_b)],
        out_specs=pl.BlockSpec((bsz, steps, MIX_W), lambda t: (0, t, 0)),
        scratch_shapes=[pltpu.VMEM((MIX_W // LANE, rows, LANE), F32),
                        pltpu.VMEM((rows, n_state), F32), pltpu.VMEM((rows, n_state), F32),
                        pltpu.VMEM((bsz, n_state), F32), pltpu.VMEM((bsz, n_state), F32)],
        compiler_params=_cparams(1),
        name="mix_s5",
    )(h, norm_w, w, bre, bim, ar, ai, cre, cim, d_row, glu_w, glu_b)


def _s5_params(a_re, a_im, log_dt, b_re, b_im, c_re, c_im, bsz):
    dt = jnp.exp(log_dt)[:, None]
    mag = jnp.exp(a_re * dt)
    abar_r, abar_i = mag * jnp.cos(a_im * dt), mag * jnp.sin(a_im * dt)
    den = a_re * a_re + a_im * a_im
    nr, ni = abar_r - 1.0, abar_i
    zr, zi = (nr * a_re + ni * a_im) / den, (ni * a_re - nr * a_im) / den
    bbar_r = zr[..., None] * b_re - zi[..., None] * b_im
    bbar_i = zr[..., None] * b_im + zi[..., None] * b_re
    n_cl = S5_GROUPS // S5_CLUSTER
    eye = jnp.eye(S5_CLUSTER, dtype=F32)

    def pack_b(bb):
        bb = bb.reshape(n_cl, S5_CLUSTER, S5_STATE, S5_GROUP)
        out = jnp.einsum("cgph,gk->cghkp", bb, eye)
        return out.reshape(n_cl, S5_CLUSTER * S5_GROUP, S5_CLUSTER * S5_STATE).astype(BF16)

    def pack_c(cc):
        cc = cc.reshape(n_cl, S5_CLUSTER, S5_GROUP, S5_STATE)
        out = jnp.einsum("cghp,gk->cgpkh", cc, eye)
        return out.reshape(n_cl, S5_CLUSTER * S5_STATE, S5_CLUSTER * S5_GROUP).astype(BF16)

    bc = lambda a: jnp.broadcast_to(a.reshape(1, -1), (bsz, S5_GROUPS * S5_STATE))
    return pack_b(bbar_r), pack_b(bbar_i), bc(abar_r), bc(abar_i), pack_c(c_re), pack_c(c_im)


def _merge_kernel(h_ref, nw_ref, wm_ref, ya_ref, yb_ref, yc_ref, yd_ref, wb_ref, wo_ref, o_ref):
    h = h_ref[...]
    xn = _rms(h, nw_ref[...]).astype(BF16)
    d = h.shape[1]
    mixed = jnp.zeros(h.shape, F32)
    for n, y_ref in enumerate((ya_ref, yb_ref, yc_ref, yd_ref)):
        gate = jax.nn.sigmoid(_dot(xn, wm_ref[:, n * d:(n + 1) * d]))
        mixed = mixed + gate * _dot(y_ref[...], wb_ref[n])
    o_ref[...] = h + _dot(mixed.astype(BF16), wo_ref[...])


def _merge_call(h, norm_w, w_m, y_a, y_b, y_c, y_d, w_b, w_o):
    bsz, seq, d = h.shape
    tile = min(ROW_TILE, seq)
    full = lambda a: pl.BlockSpec(a.shape, lambda b, t: (0,) * a.ndim)
    ytile = pl.BlockSpec((None, tile, MIX_W), lambda b, t: (b, t, 0))
    return pl.pallas_call(
        _merge_kernel,
        out_shape=jax.ShapeDtypeStruct(h.shape, F32),
        grid=(bsz, seq // tile),
        in_specs=[pl.BlockSpec((None, tile, d), lambda b, t: (b, t, 0)), full(norm_w), full(w_m),
                  ytile, ytile, ytile, ytile, full(w_b), full(w_o)],
        out_specs=pl.BlockSpec((None, tile, d), lambda b, t: (b, t, 0)),
        compiler_params=_cparams(2),
        name="merge",
    )(h, norm_w, w_m, y_a, y_b, y_c, y_d, w_b, w_o)


def _ffn_kernel(f_tiles, h_ref, nw_ref, wg_ref, wu_ref, wd_ref, o_ref):
    h = h_ref[...]
    hn = _rms(h, nw_ref[...]).astype(BF16)
    tf = wg_ref.shape[1] // f_tiles
    out = h
    for j in range(f_tiles):
        cols = slice(j * tf, (j + 1) * tf)
        act = (jax.nn.silu(_dot(hn, wg_ref[:, cols])) * _dot(hn, wu_ref[:, cols])).astype(BF16)
        out = out + _dot(act, wd_ref[cols, :])
    o_ref[...] = out


def _ffn_call(h2, norm_w, w_gate, w_up, w_down, f_tiles):
    n_tok, d = h2.shape
    tile = min(ROW_TILE, n_tok)
    resident = lambda a: pl.BlockSpec(a.shape, lambda i: (0, 0), pipeline_mode=pl.Buffered(1))
    return pl.pallas_call(
        functools.partial(_ffn_kernel, f_tiles),
        out_shape=jax.ShapeDtypeStruct(h2.shape, F32),
        grid=(n_tok // tile,),
        in_specs=[pl.BlockSpec((tile, d), lambda i: (i, 0)),
                  pl.BlockSpec(norm_w.shape, lambda i: (0, 0)),
                  resident(w_gate), resident(w_up), resident(w_down)],
        out_specs=pl.BlockSpec((tile, d), lambda i: (i, 0)),
        compiler_params=_cparams(1),
        name="ffn_dense",
    )(h2, norm_w, w_gate, w_up, w_down)


def _route_kernel(h_ref, nw_ref, rt_ref, hn_ref, idx_ref, gate_ref):
    hn = _rms(h_ref[...], nw_ref[...])
    hn_ref[...] = hn.astype(BF16)
    logits = jnp.dot(hn, rt_ref[...], preferred_element_type=F32, precision=HIGHEST)
    lg = logits.T[0:N_EXPERTS, :]
    row = lax.broadcasted_iota(jnp.int32, lg.shape, 0)
    m1 = jnp.max(lg, axis=0, keepdims=True)
    i1 = jnp.min(jnp.where(lg == m1, row, N_EXPERTS), axis=0, keepdims=True)
    lg2 = jnp.where(row == i1, -jnp.inf, lg)
    m2 = jnp.max(lg2, axis=0, keepdims=True)
    i2 = jnp.min(jnp.where(lg2 == m2, row, N_EXPERTS), axis=0, keepdims=True)
    e2 = jnp.exp(m2 - m1)
    g1 = 1.0 / (1.0 + e2)
    g2 = e2 / (1.0 + e2)
    idx_ref[...] = jnp.where(row == 0, i1, jnp.where(row == 1, i2, 0))
    gate_ref[...] = jnp.where(row == 0, g1, jnp.where(row == 1, g2, 0.0))


def _route_call(h2, norm_w, router_pad):
    n_tok, d = h2.shape
    tile = min(ROW_TILE, n_tok)
    return pl.pallas_call(
        _route_kernel,
        out_shape=(jax.ShapeDtypeStruct((n_tok, d), BF16),
                   jax.ShapeDtypeStruct((N_EXPERTS, n_tok), jnp.int32),
                   jax.ShapeDtypeStruct((N_EXPERTS, n_tok), F32)),
        grid=(n_tok // tile,),
        in_specs=[pl.BlockSpec((tile, d), lambda i: (i, 0)),
                  pl.BlockSpec(norm_w.shape, lambda i: (0, 0)),
                  pl.BlockSpec(router_pad.shape, lambda i: (0, 0))],
        out_specs=(pl.BlockSpec((tile, d), lambda i: (i, 0)),
                   pl.BlockSpec((N_EXPERTS, tile), lambda i: (0, i)),
                   pl.BlockSpec((N_EXPERTS, tile), lambda i: (0, i))),
        compiler_params=_cparams(1),
        name="moe_route",
    )(h2, norm_w, router_pad)


def _expert_kernel(be_ref, bv_ref, x_ref, wg_ref, wu_ref, wd_ref, o_ref, acc_ref):
    i, j = pl.program_id(0), pl.program_id(1)
    last = pl.num_programs(1) - 1

    @pl.when(j == 0)
    def _():
        acc_ref[...] = jnp.zeros_like(acc_ref)

    @pl.when(bv_ref[i] != 0)
    def _():
        x = x_ref[...]
        act = (jax.nn.silu(_dot(x, wg_ref[...])) * _dot(x, wu_ref[...])).astype(BF16)
        acc_ref[...] += _dot(act, wd_ref[...])

    @pl.when(j == last)
    def _():
        o_ref[...] = acc_ref[...].astype(o_ref.dtype)


def _expert_call(blk_e, blk_valid, xs, w_gate, w_up, w_down, f_tiles):
    n_rows, d = xs.shape
    f = w_gate.shape[2]
    tf = f // f_tiles
    n_blk = n_rows // MOE_TILE
    grid_spec = pltpu.PrefetchScalarGridSpec(
        num_scalar_prefetch=2,
        grid=(n_blk, f_tiles),
        in_specs=[pl.BlockSpec((MOE_TILE, d), lambda i, j, be, bv: (i, 0)),
                  pl.BlockSpec((None, d, tf), lambda i, j, be, bv: (be[i], 0, j)),
                  pl.BlockSpec((None, d, tf), lambda i, j, be, bv: (be[i], 0, j)),
                  pl.BlockSpec((None, tf, d), lambda i, j, be, bv: (be[i], j, 0))],
        out_specs=pl.BlockSpec((MOE_TILE, d), lambda i, j, be, bv: (i, 0)),
        scratch_shapes=[pltpu.VMEM((MOE_TILE, d), F32)],
    )
    return pl.pallas_call(
        _expert_kernel,
        out_shape=jax.ShapeDtypeStruct((n_rows, d), BF16),
        grid_spec=grid_spec,
        compiler_params=_cparams(2),
        name="moe_experts",
    )(blk_e, blk_valid, xs, w_gate, w_up, w_down)


def _combine_kernel(apply_norm, h_ref, y0_ref, y1_ref, g0_ref, g1_ref, nw_ref, o_ref):
    out = h_ref[...] + (g0_ref[...] * y0_ref[...].astype(F32) + g1_ref[...] * y1_ref[...].astype(F32))
    if apply_norm:
        out = _rms(out, nw_ref[...])
    o_ref[...] = out


def _combine_call(h2, y0, y1, g0, g1, norm_w, apply_norm):
    n_tok, d = h2.shape
    tile = min(ROW_TILE, n_tok)
    row = pl.BlockSpec((tile, d), lambda i: (i, 0))
    col = pl.BlockSpec((tile, 1), lambda i: (i, 0))
    return pl.pallas_call(
        functools.partial(_combine_kernel, apply_norm),
        out_shape=jax.ShapeDtypeStruct(h2.shape, F32),
        grid=(n_tok // tile,),
        in_specs=[row, row, row, col, col, pl.BlockSpec(norm_w.shape, lambda i: (0, 0))],
        out_specs=row,
        compiler_params=_cparams(1),
        name="moe_combine",
    )(h2, y0, y1, g0, g1, norm_w)


def _final_norm_kernel(h_ref, nw_ref, o_ref):
    o_ref[...] = _rms(h_ref[...], nw_ref[...])


def _final_norm_call(h2, norm_w):
    n_tok, d = h2.shape
    tile = min(ROW_TILE, n_tok)
    row = pl.BlockSpec((tile, d), lambda i: (i, 0))
    return pl.pallas_call(
        _final_norm_kernel,
        out_shape=jax.ShapeDtypeStruct(h2.shape, F32),
        grid=(n_tok // tile,),
        in_specs=[row, pl.BlockSpec(norm_w.shape, lambda i: (0, 0))],
        out_specs=row,
        compiler_params=_cparams(1),
        name="final_norm",
    )(h2, norm_w)


def _moe(h2, norm_w, router, w_gate, w_up, w_down, final_w, apply_final):
    n_tok, d = h2.shape
    router_pad = jnp.zeros((d, LANE), F32).at[:, :N_EXPERTS].set(router)
    hn, idx, gate = _route_call(h2, norm_w, router_pad)
    e_flat = idx[:TOP_K].reshape(-1)
    onehot = (e_flat[:, None] == jnp.arange(N_EXPERTS, dtype=jnp.int32)[None, :]).astype(jnp.int32)
    csum = jnp.cumsum(onehot, axis=0)
    rank = jnp.sum(csum * onehot, axis=1) - 1
    counts = csum[-1]
    padded = (counts + MOE_TILE - 1) // MOE_TILE * MOE_TILE
    pad_end = jnp.cumsum(padded)
    pad_start = pad_end - padded
    dest = pad_start[e_flat] + rank
    n_assign = n_tok * TOP_K
    n_blk = -(-n_assign // MOE_TILE) + N_EXPERTS
    blk_start = jnp.arange(n_blk, dtype=jnp.int32) * MOE_TILE
    blk_e = jnp.minimum(jnp.searchsorted(pad_end, blk_start, side="right"), N_EXPERTS - 1).astype(jnp.int32)
    blk_valid = (blk_start < pad_end[-1]).astype(jnp.int32)
    order = jnp.argsort(e_flat, stable=True).astype(jnp.int32)
    tok_sorted = jnp.concatenate([order % n_tok, jnp.zeros((n_assign,), jnp.int32)])
    grp_start = jnp.cumsum(counts) - counts
    buf = jnp.zeros((n_blk * MOE_TILE + n_assign,), jnp.int32)
    for e in range(N_EXPERTS):
        run = lax.dynamic_slice(tok_sorted, (grp_start[e],), (n_assign,))
        buf = lax.dynamic_update_slice(buf, run, (pad_start[e],))
    src_tok = buf[:n_blk * MOE_TILE]
    xs = hn.at[src_tok].get(mode="promise_in_bounds")
    f_tiles = 2 if w_gate.shape[2] % (2 * LANE) == 0 else 1
    ys = _expert_call(blk_e, blk_valid, xs, w_gate, w_up, w_down, f_tiles)
    y0 = ys.at[dest[:n_tok]].get(mode="promise_in_bounds")
    y1 = ys.at[dest[n_tok:]].get(mode="promise_in_bounds")
    g0 = gate[0].reshape(n_tok, 1)
    g1 = gate[1].reshape(n_tok, 1)
    return _combine_call(h2, y0, y1, g0, g1, final_w, apply_final)


def _pad_cols(w, n):
    return jnp.concatenate([w, jnp.zeros((w.shape[0], n - w.shape[1]), w.dtype)], axis=1)


def _token_mix(h, norm_w, w_in, lb, hgrn_norm, s5, s5_d, s5_glu_w, s5_glu_b, dn_conv_w, dn_a_log, dn_dt_bias,
               dn_norm, gla_w_lr, gla_b_lr, gla_norm, w_branch, w_out, rs, tril):
    bsz, seq, d = h.shape
    m = MIX_W
    o = 0
    w_a = w_in[:, o:o + 4 * m]; o += 4 * m
    w_s5 = w_in[:, o:o + m]; o += m
    w_qkv = w_in[:, o:o + 3 * m]; o += 3 * m
    w_bd = w_in[:, o:o + 2 * DN_HEADS]; o += 2 * DN_HEADS
    w_dg = w_in[:, o:o + m]; o += m
    nqk = GLA_HEADS * GLA_DK
    w_gqk = w_in[:, o:o + 2 * nqk]; o += 2 * nqk
    w_gv = w_in[:, o:o + m]; o += m
    w_glr = w_in[:, o:o + GLA_RANK]; o += GLA_RANK
    w_gg = w_in[:, o:o + m]; o += m
    w_m = w_in[:, o:o + N_BRANCH * d]

    w_a = w_a.astype(BF16)
    w_c = jnp.concatenate([w_qkv, w_dg, _pad_cols(w_bd, LANE)], axis=1).astype(BF16)
    w_d = jnp.concatenate([w_gqk, w_gv, w_gg, _pad_cols(w_glr, LANE)], axis=1).astype(BF16)
    row = lambda a: a.reshape(1, -1).astype(F32)
    nw = row(norm_w)

    dummy = jnp.zeros((1, LANE), F32)
    y_a = _gla_call("hgrn", h, nw, w_a, row(lb), dummy, row(hgrn_norm), rs)

    wlr_pad = jnp.zeros((LANE, nqk), F32).at[:GLA_RANK].set(gla_w_lr)
    y_d = _gla_call("gla", h, nw, w_d, wlr_pad, row(gla_b_lr), row(gla_norm), rs)

    lane_row = lambda a: jnp.zeros((1, LANE), F32).at[0, DN_HEADS:2 * DN_HEADS].set(a)
    y_c = _delta_call(h, nw, w_c, dn_conv_w.astype(F32), lane_row(dn_a_log), lane_row(dn_dt_bias), row(dn_norm), tril)

    bre, bim, ar, ai, cre, cim = s5
    y_b = _s5_call(h, nw, w_s5.astype(BF16), bre, bim, ar, ai, cre, cim, row(s5_d), s5_glu_w.astype(BF16),
                   row(s5_glu_b))

    return _merge_call(h, nw, w_m.astype(BF16), y_a, y_b, y_c, y_d, w_branch.astype(BF16), w_out.astype(BF16))


def kernel(x, attn_norm, ffn_norm, final_norm, w_in, hgrn_lb, hgrn_norm, s5_a_re, s5_a_im, s5_log_dt, s5_b_re, s5_b_im, s5_c_re, s5_c_im, s5_d, s5_glu_w, s5_glu_b, dn_conv_w, dn_a_log, dn_dt_bias, dn_norm, gla_w_lr, gla_b_lr, gla_norm, w_branch, w_out, ffn_w_gate, ffn_w_up, ffn_w_down, moe_router, moe_w_gate, moe_w_up, moe_w_down):
    bsz, seq, d = x.shape
    depth = w_in.shape[0]
    assert bsz == SUBLANE, "the S5 kernel keeps one batch row per sublane"
    assert seq % CHUNK == 0 and d == D_MODEL
    lb_soft = jax.nn.softmax(hgrn_lb.astype(F32), axis=0)
    lb_all = jnp.cumsum(lb_soft, axis=0) - lb_soft[0]
    rs = jnp.asarray(_range_sum_matrix(), BF16)
    tril = jnp.asarray(np.tile(np.tril(np.ones((CHUNK, CHUNK), np.float32)), (1, 3)), BF16)
    row = lambda a: a.reshape(1, -1).astype(F32)

    h = x
    for layer in range(depth):
        s5 = _s5_params(s5_a_re[layer], s5_a_im[layer], s5_log_dt[layer], s5_b_re[layer], s5_b_im[layer],
                        s5_c_re[layer], s5_c_im[layer], bsz)
        h = _token_mix(h, attn_norm[layer], w_in[layer], lb_all[layer], hgrn_norm[layer], s5, s5_d[layer],
                       s5_glu_w[layer], s5_glu_b[layer], dn_conv_w[layer], dn_a_log[layer], dn_dt_bias[layer],
                       dn_norm[layer], gla_w_lr[layer], gla_b_lr[layer], gla_norm[layer], w_branch[layer],
                       w_out[layer], rs, tril)
        h2 = h.reshape(bsz * seq, d)
        j = layer // 2
        last = layer == depth - 1
        if layer % 2 == 0:
            f = ffn_w_gate.shape[2]
            f_tiles = 2 if f % (2 * LANE) == 0 else 1
            h2 = _ffn_call(h2, row(ffn_norm[layer]), ffn_w_gate[j].astype(BF16), ffn_w_up[j].astype(BF16),
                           ffn_w_down[j].astype(BF16), f_tiles)
            if last:
                h2 = _final_norm_call(h2, row(final_norm))
        else:
            h2 = _moe(h2, row(ffn_norm[layer]), moe_router[j], moe_w_gate[j].astype(BF16),
                      moe_w_up[j].astype(BF16), moe_w_down[j].astype(BF16), row(final_norm), last)
        h = h2.reshape(bsz, seq, d)
    return h
```

```python
import functools

import numpy as np
import jax
import jax.numpy as jnp
from jax import lax
from jax.experimental import pallas as pl
from jax.experimental.pallas import tpu as pltpu

F32 = jnp.float32
BF16 = jnp.bfloat16
HIGHEST = lax.Precision.HIGHEST

D_MODEL = 1024
MIX_W = 512
HG_HEADS = 4
HG_DH = MIX_W // HG_HEADS
S5_GROUP = 16
S5_GROUPS = MIX_W // S5_GROUP
S5_STATE = 64
DN_HEADS = 4
DN_DH = MIX_W // DN_HEADS
CONV_K = 4
GLA_HEADS = 4
GLA_DK = 64
GLA_DV = MIX_W // GLA_HEADS
GLA_RANK = 16
GLA_NORMALIZER = 16.0
CHUNK = 64
N_BRANCH = 4
N_EXPERTS = 8
TOP_K = 2
NORM_EPS = 1e-6

LANE = 128
SUBLANE = 8
VMEM_LIMIT = 56 * 1024 * 1024

MIX_TILE = 512
S5_STEPS = 128
S5_CLUSTER = 8
ROW_TILE = 512
MOE_TILE = 512
LEVELS = (32, 16, 8, 4, 2, 1)


def _cparams(n_axes):
    return pltpu.CompilerParams(dimension_semantics=("arbitrary",) * n_axes, vmem_limit_bytes=VMEM_LIMIT)


def _rms(x, w):
    return x * lax.rsqrt(jnp.mean(x * x, axis=-1, keepdims=True) + NORM_EPS) * w


def _dot(a, b):
    return jnp.dot(a, b, preferred_element_type=F32)


def _dot_nt(a, b):
    return lax.dot_general(a, b, (((1,), (1,)), ((), ())), preferred_element_type=F32)


def _dot_tn(a, b):
    return lax.dot_general(a, b, (((0,), (0,)), ((), ())), preferred_element_type=F32)


def _split3(x):
    hi = x.astype(BF16)
    r1 = x - hi.astype(F32)
    mid = r1.astype(BF16)
    lo = (r1 - mid.astype(F32)).astype(BF16)
    return hi, mid, lo


def _split2(x):
    hi = x.astype(BF16)
    return hi, (x - hi.astype(F32)).astype(BF16)


def _lhs3(hi, lo):
    return jnp.concatenate([hi, hi, lo], axis=1)


def _rhs3(hi, lo):
    return jnp.concatenate([hi, lo, hi], axis=0)


def _mm3(lhs3, rhs3):
    return jnp.dot(lhs3, rhs3, preferred_element_type=F32)


def _range_sum_matrix():
    c = CHUNK
    m = np.zeros((2 + len(LEVELS), c, c), np.float32)
    for r in range(c):
        m[0, r, :r + 1] = 1.0
        m[1, r, r + 1:] = 1.0
        for l, b in enumerate(LEVELS):
            mid = (r // (2 * b)) * 2 * b + b - 1
            if r > mid:
                m[2 + l, r, mid + 1:r + 1] = 1.0
            else:
                m[2 + l, r, r + 1:mid + 1] = 1.0
    m = m.reshape(-1, c)
    return np.concatenate([m, m, m], axis=1)


def _head_norm_gate(o, w, gate):
    return o * lax.rsqrt(jnp.mean(o * o, axis=-1, keepdims=True) + NORM_EPS) * w * jax.nn.silu(gate)


def _gla_chunks(n_heads, dk, dv, q_ref, k_ref, v_ref, gs_ref, rs_ref, o_ref, st_ref, n_chunks):
    ii = lax.broadcasted_iota(jnp.int32, (CHUNK, CHUNK), 0)
    jj = lax.broadcasted_iota(jnp.int32, (CHUNK, CHUNK), 1)
    masks = [((ii // (2 * b)) == (jj // (2 * b))) & ((ii % (2 * b)) >= b) & ((jj % (2 * b)) < b) for b in LEVELS]
    chunks = range(n_chunks)
    heads = range(n_heads)
    rows = [slice(c * CHUNK, (c + 1) * CHUNK) for c in chunks]

    xs = []
    for c in chunks:
        g3 = jnp.concatenate([gs_ref[0, rows[c], :], gs_ref[1, rows[c], :], gs_ref[2, rows[c], :]], axis=0)
        xs.append(jnp.exp(_dot(rs_ref[...], g3)))

    scores, kv, qds, decs, diags = {}, {}, {}, {}, {}
    for c in chunks:
        qc, kc, vc = q_ref[rows[c], :], k_ref[rows[c], :], v_ref[rows[c], :]
        x = xs[c]
        for h in heads:
            cols = slice(h * dk, (h + 1) * dk)
            qh, kh = qc[:, cols], kc[:, cols]
            vh = vc[:, h * dv:(h + 1) * dv]
            a = jnp.zeros((CHUNK, CHUNK), F32)
            for l in range(len(LEVELS)):
                xl = x[(2 + l) * CHUNK:(3 + l) * CHUNK, cols]
                al = _dot_nt((qh * xl).astype(BF16), (kh * xl).astype(BF16))
                a = jnp.where(masks[l], al, a)
            scores[c, h] = a.astype(BF16)
            kv[c, h] = _dot_tn(vh.astype(BF16), (kh * x[CHUNK:2 * CHUNK, cols]).astype(BF16))
            qds[c, h] = (qh * x[0:CHUNK, cols]).astype(BF16)
            decs[c, h] = x[CHUNK - 1:CHUNK, cols]
            diags[c, h] = jnp.sum(qh * kh, axis=-1, keepdims=True) * vh

    for c in chunks:
        vc = v_ref[rows[c], :]
        for h in heads:
            vb = vc[:, h * dv:(h + 1) * dv].astype(BF16)
            o_ref[rows[c], h * dv:(h + 1) * dv] = _dot(scores[c, h], vb) + diags[c, h]

    for h in heads:
        st = st_ref[h]
        for c in chunks:
            o_ref[rows[c], h * dv:(h + 1) * dv] += _dot_nt(qds[c, h], st.astype(BF16))
            st = st * decs[c, h] + kv[c, h]
        st_ref[h] = st


def _gla_kernel(mode, h_ref, nw_ref, w_ref, aux1_ref, aux2_ref, hw_ref, rs_ref, y_ref,
                p_ref, q_ref, k_ref, gs_ref, o_ref, st_ref):
    if mode == "hgrn":
        n_heads, dk, dv = HG_HEADS, HG_DH, HG_DH
    else:
        n_heads, dk, dv = GLA_HEADS, GLA_DK, GLA_DV
    n = n_heads * dk
    tile = h_ref.shape[0]

    @pl.when(pl.program_id(1) == 0)
    def _():
        st_ref[...] = jnp.zeros_like(st_ref)

    xn = _rms(h_ref[...], nw_ref[...]).astype(BF16)
    p_ref[...] = _dot(xn, w_ref[...])
    if mode == "hgrn":
        lb = aux1_ref[...]
        z = p_ref[:, n:2 * n]
        g = jnp.log(lb + (1.0 - lb) * jax.nn.sigmoid(z))
        q_ref[...] = p_ref[:, 0:n] * dk ** -0.5
        k_ref[...] = (1.0 - lb) * jax.nn.sigmoid(-z)
        v_off = 2 * n
    else:
        v_off = 2 * n
        glr = p_ref[:, v_off + 2 * MIX_W:v_off + 2 * MIX_W + LANE]
        logit = jnp.dot(glr, aux1_ref[...], preferred_element_type=F32, precision=HIGHEST) + aux2_ref[...]
        g = jax.nn.log_sigmoid(logit) / GLA_NORMALIZER
        q_ref[...] = p_ref[:, 0:n] * dk ** -0.5
        k_ref[...] = p_ref[:, n:2 * n]
    hi, mid, lo = _split3(g)
    gs_ref[0] = hi
    gs_ref[1] = mid
    gs_ref[2] = lo

    _gla_chunks(n_heads, dk, dv, q_ref, k_ref, p_ref.at[:, v_off:v_off + MIX_W], gs_ref, rs_ref, o_ref, st_ref,
                tile // CHUNK)

    g_off = v_off + MIX_W
    for h in range(n_heads):
        cols = slice(h * dv, (h + 1) * dv)
        y = _head_norm_gate(o_ref[:, cols], hw_ref[...], p_ref[:, g_off + h * dv:g_off + (h + 1) * dv])
        y_ref[:, cols] = y.astype(y_ref.dtype)


def _gla_call(mode, h, norm_w, w, aux1, aux2, head_w, rs):
    bsz, seq, d = h.shape
    n_heads, dk, dv = (HG_HEADS, HG_DH, HG_DH) if mode == "hgrn" else (GLA_HEADS, GLA_DK, GLA_DV)
    n = n_heads * dk
    tile = min(MIX_TILE, seq)
    ncols = w.shape[1]
    full = lambda a: pl.BlockSpec(a.shape, lambda b, t: (0,) * a.ndim)
    return pl.pallas_call(
        functools.partial(_gla_kernel, mode),
        out_shape=jax.ShapeDtypeStruct((bsz, seq, MIX_W), BF16),
        grid=(bsz, seq // tile),
        in_specs=[pl.BlockSpec((None, tile, d), lambda b, t: (b, t, 0)),
                  full(norm_w), full(w), full(aux1), full(aux2), full(head_w), full(rs)],
        out_specs=pl.BlockSpec((None, tile, MIX_W), lambda b, t: (b, t, 0)),
        scratch_shapes=[pltpu.VMEM((tile, ncols), F32), pltpu.VMEM((tile, n), F32), pltpu.VMEM((tile, n), F32),
                        pltpu.VMEM((3, tile, n), BF16), pltpu.VMEM((tile, MIX_W), F32),
                        pltpu.VMEM((n_heads, dv, dk), F32)],
        compiler_params=_cparams(2),
        name="mix_" + mode,
    )(h, norm_w, w, aux1, aux2, head_w, rs)


def _delta_kernel(h_ref, nw_ref, w_ref, cw_ref, alog_ref, dtb_ref, hw_ref, tril_ref, y_ref,
                  p_ref, xs_ref, q_ref, k_ref, v_ref, sm_ref, o_ref, st_ref):
    tile = h_ref.shape[0]
    nh, dh = DN_HEADS, DN_DH
    qkv = 3 * MIX_W

    @pl.when(pl.program_id(1) == 0)
    def _():
        st_ref[...] = jnp.zeros_like(st_ref)
        xs_ref[0:SUBLANE, :] = jnp.zeros((SUBLANE, qkv), F32)

    xn = _rms(h_ref[...], nw_ref[...]).astype(BF16)
    p_ref[...] = _dot(xn, w_ref[...])

    xs_ref[SUBLANE:SUBLANE + tile, :] = p_ref[:, 0:qkv]
    conv = cw_ref[CONV_K - 1:CONV_K, :] * xs_ref[SUBLANE:SUBLANE + tile, :]
    for j in range(1, CONV_K):
        conv = conv + cw_ref[CONV_K - 1 - j:CONV_K - j, :] * xs_ref[SUBLANE - j:SUBLANE - j + tile, :]
    xs_ref[0:SUBLANE, :] = xs_ref[tile:tile + SUBLANE, :]
    conv = jax.nn.silu(conv)
    for h in range(nh):
        cols = slice(h * dh, (h + 1) * dh)
        qh = conv[:, h * dh:(h + 1) * dh]
        kh = conv[:, MIX_W + h * dh:MIX_W + (h + 1) * dh]
        q_ref[:, cols] = qh * lax.rsqrt(jnp.sum(qh * qh, axis=-1, keepdims=True) + NORM_EPS) * dh ** -0.5
        k_ref[:, cols] = kh * lax.rsqrt(jnp.sum(kh * kh, axis=-1, keepdims=True) + NORM_EPS)
    v_ref[...] = conv[:, 2 * MIX_W:3 * MIX_W]

    small = p_ref[:, qkv + MIX_W:qkv + MIX_W + LANE]
    gdec = -jnp.exp(alog_ref[...]) * jax.nn.softplus(small + dtb_ref[...])
    hi, mid, lo = _split3(gdec)
    sm_ref[0] = hi
    sm_ref[1] = mid
    sm_ref[2] = lo

    pw_ = nh * CHUNK
    ii = lax.broadcasted_iota(jnp.int32, (CHUNK, pw_), 0)
    lane_p = lax.broadcasted_iota(jnp.int32, (CHUNK, pw_), 1)
    jj = lane_p % CHUNK
    incl = ii >= jj
    strict = ii > jj
    eye = (ii == jj).astype(F32)
    head_p = (lane_p // CHUNK).astype(F32).astype(BF16)
    head_w = (lax.broadcasted_iota(jnp.int32, (CHUNK, MIX_W), 1) // dh).astype(F32).astype(BF16)
    sel_p = [head_p == h for h in range(nh)]
    sel_w = [head_w == h for h in range(nh)]
    n_chunks = tile // CHUNK

    def block_diag(y, sel):
        return jnp.concatenate([jnp.where(sel[h], y, jnp.zeros_like(y)) for h in range(nh)], axis=0)

    def head_cols(x, lo, width):
        return jnp.concatenate([jnp.broadcast_to(x[:, lo + h:lo + h + 1], (x.shape[0], width)) for h in range(nh)],
                               axis=1)

    gcums = []
    for c in range(n_chunks):
        rows = slice(c * CHUNK, (c + 1) * CHUNK)
        g3 = jnp.concatenate([sm_ref[0, rows, :], sm_ref[1, rows, :], sm_ref[2, rows, :]], axis=0)
        gcums.append(_dot(tril_ref[...], g3))
    prob = []
    for c in range(n_chunks):
        rows = slice(c * CHUNK, (c + 1) * CHUNK)
        gcum = gcums[c]
        gcum_t = gcum.T
        beta_c = jax.nn.sigmoid(p_ref[rows, qkv + MIX_W:qkv + MIX_W + LANE])
        qc, kc, vc = q_ref[rows, :], k_ref[rows, :], v_ref[rows, :]
        gcol_p = head_cols(gcum, nh, CHUNK)
        grow_p = jnp.concatenate([gcum_t[nh + h:nh + h + 1, :] for h in range(nh)], axis=1)
        decay = jnp.exp(jnp.where(incl, gcol_p - grow_p, -jnp.inf))
        gcol_w = head_cols(gcum, nh, dh)
        glast_w = gcol_w[CHUNK - 1:CHUNK, :]
        beta_w = head_cols(beta_c, 0, dh)
        egc = jnp.exp(gcol_w)
        kb = kc * beta_w
        k_bd = block_diag(kc.astype(BF16), sel_w)
        kk_qk = _dot_nt(jnp.concatenate([kb.astype(BF16), qc.astype(BF16)], axis=0), k_bd)
        lower = jnp.where(strict, kk_qk[0:CHUNK] * decay, 0.0)
        prob.append(dict(
            lower=lower, vb_bd=block_diag((vc * beta_w).astype(BF16), sel_w),
            kbg_bd=block_diag((kb * egc).astype(BF16), sel_w),
            qk=(kk_qk[CHUNK:2 * CHUNK] * decay).astype(BF16), q_dec=(qc * egc).astype(BF16),
            k_dec=(kc * jnp.exp(glast_w - gcol_w)).astype(BF16), a=jnp.exp(glast_w)))

    def rhs3_bd(hi, lo):
        bh, bl = block_diag(hi, sel_p), block_diag(lo, sel_p)
        return jnp.concatenate([bh, bl, bh], axis=0)

    t_inv = [eye - pr["lower"] for pr in prob]
    pw = []
    for pr in prob:
        hi, lo = _split2(pr["lower"])
        pw.append(_mm3(_lhs3(hi, lo), rhs3_bd(hi, lo)))
    for it in range(5):
        for i in range(n_chunks):
            hi, lo = _split2(pw[i])
            lhs_t = _lhs3(*_split2(t_inv[i]))
            if it < 4:
                both = _mm3(jnp.concatenate([lhs_t, _lhs3(hi, lo)], axis=0), rhs3_bd(hi, lo))
                t_inv[i] = t_inv[i] + both[0:CHUNK]
                pw[i] = both[CHUNK:2 * CHUNK]
            else:
                t_inv[i] = t_inv[i] + _mm3(lhs_t, rhs3_bd(hi, lo))
    for pr, t in zip(prob, t_inv):
        tb = t.astype(BF16)
        pr["u"] = _dot(tb, pr["vb_bd"])
        pr["w"] = _dot(tb, pr["kbg_bd"]).astype(BF16)

    st = [st_ref[h] for h in range(nh)]
    for c in range(n_chunks):
        rows = slice(c * CHUNK, (c + 1) * CHUNK)
        pr = prob[c]
        hcols = [slice(h * dh, (h + 1) * dh) for h in range(nh)]
        sb = [st[h].astype(BF16) for h in range(nh)]
        ws = [_dot(pr["w"][:, hcols[h]], sb[h]) for h in range(nh)]
        qs = [_dot(pr["q_dec"][:, hcols[h]], sb[h]) for h in range(nh)]
        vnb = [(pr["u"][:, hcols[h]] - ws[h]).astype(BF16) for h in range(nh)]
        for h in range(nh):
            o_ref[rows, hcols[h]] = qs[h] + _dot(pr["qk"][:, h * CHUNK:(h + 1) * CHUNK], vnb[h])
        for h in range(nh):
            st[h] = st[h] * pr["a"][:, hcols[h]] + _dot_tn(pr["k_dec"][:, hcols[h]], vnb[h])
    for h in range(nh):
        st_ref[h] = st[h]

    g_off = qkv
    for h in range(nh):
        cols = slice(h * dh, (h + 1) * dh)
        y = _head_norm_gate(o_ref[:, cols], hw_ref[...], p_ref[:, g_off + h * dh:g_off + (h + 1) * dh])
        y_ref[:, cols] = y.astype(y_ref.dtype)


def _delta_call(h, norm_w, w, conv_w, alog_row, dtb_row, head_w, tril):
    bsz, seq, d = h.shape
    tile = min(MIX_TILE, seq)
    ncols = w.shape[1]
    full = lambda a: pl.BlockSpec(a.shape, lambda b, t: (0,) * a.ndim)
    return pl.pallas_call(
        _delta_kernel,
        out_shape=jax.ShapeDtypeStruct((bsz, seq, MIX_W), BF16),
        grid=(bsz, seq // tile),
        in_specs=[pl.BlockSpec((None, tile, d), lambda b, t: (b, t, 0)),
                  full(norm_w), full(w), full(conv_w), full(alog_row), full(dtb_row), full(head_w), full(tril)],
        out_specs=pl.BlockSpec((None, tile, MIX_W), lambda b, t: (b, t, 0)),
        scratch_shapes=[pltpu.VMEM((tile, ncols), F32), pltpu.VMEM((tile + SUBLANE, 3 * MIX_W), F32),
                        pltpu.VMEM((tile, MIX_W), F32), pltpu.VMEM((tile, MIX_W), F32),
                        pltpu.VMEM((tile, MIX_W), F32), pltpu.VMEM((3, tile, LANE), BF16),
                        pltpu.VMEM((tile, MIX_W), F32), pltpu.VMEM((DN_HEADS, DN_DH, DN_DH), F32)],
        compiler_params=_cparams(2),
        name="mix_delta",
    )(h, norm_w, w, conv_w, alog_row, dtb_row, head_w, tril)


def _s5_kernel(h_ref, nw_ref, w_ref, bre_ref, bim_ref, ar_ref, ai_ref, cre_ref, cim_ref, d_ref, gw_ref, gb_ref,
               y_ref, u_ref, xr_ref, xi_ref, sr_ref, si_ref):
    bsz, steps, d = h_ref.shape
    rows = bsz * steps
    n_cl = bre_ref.shape[0]
    cw = bre_ref.shape[1]
    sw = bre_ref.shape[2]

    @pl.when(pl.program_id(0) == 0)
    def _():
        sr_ref[...] = jnp.zeros_like(sr_ref)
        si_ref[...] = jnp.zeros_like(si_ref)

    xn = _rms(h_ref[...], nw_ref[...]).astype(BF16).reshape(rows, d)
    p = _dot(xn, w_ref[...])
    n_lt = u_ref.shape[0]
    for b in range(bsz):
        for j in range(n_lt):
            u_ref[j, pl.ds(b, steps, stride=bsz), :] = p[b * steps:(b + 1) * steps, j * LANE:(j + 1) * LANE]

    u = jnp.concatenate([u_ref[j] for j in range(n_lt)], axis=1)
    ub = u.astype(BF16)
    def input_term(c):
        uc = ub[:, c * cw:(c + 1) * cw]
        xr_ref[:, c * sw:(c + 1) * sw] = _dot(uc, bre_ref[c])
        xi_ref[:, c * sw:(c + 1) * sw] = _dot(uc, bim_ref[c])

    def scan(c):
        ls = slice(c * sw, (c + 1) * sw)
        ar, ai = ar_ref[:, ls], ai_ref[:, ls]
        xr, xi = sr_ref[:, ls], si_ref[:, ls]
        for i in range(steps):
            rr = slice(i * bsz, (i + 1) * bsz)
            xr, xi = ar * xr - ai * xi + xr_ref[rr, ls], ar * xi + ai * xr + xi_ref[rr, ls]
            xr_ref[rr, ls] = xr
            xi_ref[rr, ls] = xi
        sr_ref[:, ls] = xr
        si_ref[:, ls] = xi

    def output_term(c):
        ls = slice(c * sw, (c + 1) * sw)
        return _dot(xr_ref[:, ls].astype(BF16), cre_ref[c]) - _dot(xi_ref[:, ls].astype(BF16), cim_ref[c])

    ys = [None] * n_cl
    input_term(0)
    for c in range(n_cl):
        if c + 1 < n_cl:
            input_term(c + 1)
        if c > 0:
            ys[c - 1] = output_term(c - 1)
        scan(c)
    ys[n_cl - 1] = output_term(n_cl - 1)
    y = jnp.concatenate(ys, axis=1) + d_ref[...] * u
    z = jax.nn.gelu(y)
    gl = _dot(z.astype(BF16), gw_ref[...]) + gb_ref[...]
    out = z * jax.nn.sigmoid(gl)
    for j in range(n_lt):
        u_ref[j] = out[:, j * LANE:(j + 1) * LANE]
    for b in range(bsz):
        for j in range(n_lt):
            y_ref[b, :, j * LANE:(j + 1) * LANE] = u_ref[j, pl.ds(b, steps, stride=bsz), :].astype(y_ref.dtype)


def _s5_call(h, norm_w, w, bre, bim, ar, ai, cre, cim, d_row, glu_w, glu_b):
    bsz, seq, d = h.shape
    steps = min(S5_STEPS, seq)
    rows = steps * bsz
    n_state = ar.shape[1]
    full = lambda a: pl.BlockSpec(a.shape, lambda t: (0,) * a.ndim)
    return pl.pallas_call(
        _s5_kernel,
        out_shape=jax.ShapeDtypeStruct((bsz, seq, MIX_W), BF16),
        grid=(seq // steps,),
        in_specs=[pl.BlockSpec((bsz, steps, d), lambda t: (0, t, 0)), full(norm_w), full(w),
                  full(bre), full(bim), full(ar), full(ai), full(cre), full(cim), full(d_row), full(glu_w),
                  full(glu_b)],
        out_specs=pl.BlockSpec((bsz, steps, MIX_W), lambda t: (0, t, 0)),
        scratch_shapes=[pltpu.VMEM((MIX_W // LANE, rows, LANE), F32),
                        pltpu.VMEM((rows, n_state), F32), pltpu.VMEM((rows, n_state), F32),
                        pltpu.VMEM((bsz, n_state), F32), pltpu.VMEM((bsz, n_state), F32)],
        compiler_params=_cparams(1),
        name="mix_s5",
    )(h, norm_w, w, bre, bim, ar, ai, cre, cim, d_row, glu_w, glu_b)


def _s5_params(a_re, a_im, log_dt, b_re, b_im, c_re, c_im, bsz):
    dt = jnp.exp(log_dt)[:, None]
    mag = jnp.exp(a_re * dt)
    abar_r, abar_i = mag * jnp.cos(a_im * dt), mag * jnp.sin(a_im * dt)
    den = a_re * a_re + a_im * a_im
    nr, ni = abar_r - 1.0, abar_i
    zr, zi = (nr * a_re + ni * a_im) / den, (ni * a_re - nr * a_im) / den
    bbar_r = zr[..., None] * b_re - zi[..., None] * b_im
    bbar_i = zr[..., None] * b_im + zi[..., None] * b_re
    n_cl = S5_GROUPS // S5_CLUSTER
    eye = jnp.eye(S5_CLUSTER, dtype=F32)

    def pack_b(bb):
        bb = bb.reshape(n_cl, S5_CLUSTER, S5_STATE, S5_GROUP)
        out = jnp.einsum("cgph,gk->cghkp", bb, eye)
        return out.reshape(n_cl, S5_CLUSTER * S5_GROUP, S5_CLUSTER * S5_STATE).astype(BF16)

    def pack_c(cc):
        cc = cc.reshape(n_cl, S5_CLUSTER, S5_GROUP, S5_STATE)
        out = jnp.einsum("cghp,gk->cgpkh", cc, eye)
        return out.reshape(n_cl, S5_CLUSTER * S5_STATE, S5_CLUSTER * S5_GROUP).astype(BF16)

    bc = lambda a: jnp.broadcast_to(a.reshape(1, -1), (bsz, S5_GROUPS * S5_STATE))
    return pack_b(bbar_r), pack_b(bbar_i), bc(abar_r), bc(abar_i), pack_c(c_re), pack_c(c_im)


def _merge_kernel(h_ref, nw_ref, wm_ref, ya_ref, yb_ref, yc_ref, yd_ref, wb_ref, wo_ref, o_ref):
    h = h_ref[...]
    xn = _rms(h, nw_ref[...]).astype(BF16)
    d = h.shape[1]
    mixed = jnp.zeros(h.shape, F32)
    for n, y_ref in enumerate((ya_ref, yb_ref, yc_ref, yd_ref)):
        gate = jax.nn.sigmoid(_dot(xn, wm_ref[:, n * d:(n + 1) * d]))
        mixed = mixed + gate * _dot(y_ref[...], wb_ref[n])
    o_ref[...] = h + _dot(mixed.astype(BF16), wo_ref[...])


def _merge_call(h, norm_w, w_m, y_a, y_b, y_c, y_d, w_b, w_o):
    bsz, seq, d = h.shape
    tile = min(ROW_TILE, seq)
    full = lambda a: pl.BlockSpec(a.shape, lambda b, t: (0,) * a.ndim)
    ytile = pl.BlockSpec((None, tile, MIX_W), lambda b, t: (b, t, 0))
    return pl.pallas_call(
        _merge_kernel,
        out_shape=jax.ShapeDtypeStruct(h.shape, F32),
        grid=(bsz, seq // tile),
        in_specs=[pl.BlockSpec((None, tile, d), lambda b, t: (b, t, 0)), full(norm_w), full(w_m),
                  ytile, ytile, ytile, ytile, full(w_b), full(w_o)],
        out_specs=pl.BlockSpec((None, tile, d), lambda b, t: (b, t, 0)),
        compiler_params=_cparams(2),
        name="merge",
    )(h, norm_w, w_m, y_a, y_b, y_c, y_d, w_b, w_o)


def _ffn_kernel(f_tiles, h_ref, nw_ref, wg_ref, wu_ref, wd_ref, o_ref):
    h = h_ref[...]
    hn = _rms(h, nw_ref[...]).astype(BF16)
    tf = wg_ref.shape[1] // f_tiles
    out = h
    for j in range(f_tiles):
        cols = slice(j * tf, (j + 1) * tf)
        act = (jax.nn.silu(_dot(hn, wg_ref[:, cols])) * _dot(hn, wu_ref[:, cols])).astype(BF16)
        out = out + _dot(act, wd_ref[cols, :])
    o_ref[...] = out


def _ffn_call(h2, norm_w, w_gate, w_up, w_down, f_tiles):
    n_tok, d = h2.shape
    tile = min(ROW_TILE, n_tok)
    resident = lambda a: pl.BlockSpec(a.shape, lambda i: (0, 0), pipeline_mode=pl.Buffered(1))
    return pl.pallas_call(
        functools.partial(_ffn_kernel, f_tiles),
        out_shape=jax.ShapeDtypeStruct(h2.shape, F32),
        grid=(n_tok // tile,),
        in_specs=[pl.BlockSpec((tile, d), lambda i: (i, 0)),
                  pl.BlockSpec(norm_w.shape, lambda i: (0, 0)),
                  resident(w_gate), resident(w_up), resident(w_down)],
        out_specs=pl.BlockSpec((tile, d), lambda i: (i, 0)),
        compiler_params=_cparams(1),
        name="ffn_dense",
    )(h2, norm_w, w_gate, w_up, w_down)


def _route_kernel(h_ref, nw_ref, rt_ref, hn_ref, idx_ref, gate_ref):
    hn = _rms(h_ref[...], nw_ref[...])
    hn_ref[...] = hn.astype(BF16)
    hn_hi, hn_lo = _split2(hn)
    logits = (_dot(hn_hi, rt_ref[0]) + _dot(hn_hi, rt_ref[1])) + _dot(hn_lo, rt_ref[0])
    lg = logits.T[0:N_EXPERTS, :]
    row = lax.broadcasted_iota(jnp.int32, lg.shape, 0)
    m1 = jnp.max(lg, axis=0, keepdims=True)
    i1 = jnp.min(jnp.where(lg == m1, row, N_EXPERTS), axis=0, keepdims=True)
    lg2 = jnp.where(row == i1, -jnp.inf, lg)
    m2 = jnp.max(lg2, axis=0, keepdims=True)
    i2 = jnp.min(jnp.where(lg2 == m2, row, N_EXPERTS), axis=0, keepdims=True)
    e2 = jnp.exp(m2 - m1)
    g1 = 1.0 / (1.0 + e2)
    g2 = e2 / (1.0 + e2)
    idx_ref[...] = jnp.where(row == 0, i1, jnp.where(row == 1, i2, 0))
    gate_ref[...] = jnp.where(row == 0, g1, jnp.where(row == 1, g2, 0.0))


def _route_call(h2, norm_w, router_pad):
    n_tok, d = h2.shape
    tile = min(ROW_TILE, n_tok)
    return pl.pallas_call(
        _route_kernel,
        out_shape=(jax.ShapeDtypeStruct((n_tok, d), BF16),
                   jax.ShapeDtypeStruct((N_EXPERTS, n_tok), jnp.int32),
                   jax.ShapeDtypeStruct((N_EXPERTS, n_tok), F32)),
        grid=(n_tok // tile,),
        in_specs=[pl.BlockSpec((tile, d), lambda i: (i, 0)),
                  pl.BlockSpec(norm_w.shape, lambda i: (0, 0)),
                  pl.BlockSpec(router_pad.shape, lambda i: (0,) * router_pad.ndim)],
        out_specs=(pl.BlockSpec((tile, d), lambda i: (i, 0)),
                   pl.BlockSpec((N_EXPERTS, tile), lambda i: (0, i)),
                   pl.BlockSpec((N_EXPERTS, tile), lambda i: (0, i))),
        compiler_params=_cparams(1),
        name="moe_route",
    )(h2, norm_w, router_pad)


def _expert_kernel(be_ref, bv_ref, x_ref, wg_ref, wu_ref, wd_ref, o_ref, acc_ref):
    i, j = pl.program_id(0), pl.program_id(1)
    last = pl.num_programs(1) - 1

    @pl.when(j == 0)
    def _():
        acc_ref[...] = jnp.zeros_like(acc_ref)

    @pl.when(bv_ref[i] != 0)
    def _():
        x = x_ref[...]
        act = (jax.nn.silu(_dot(x, wg_ref[...])) * _dot(x, wu_ref[...])).astype(BF16)
        acc_ref[...] += _dot(act, wd_ref[...])

    @pl.when(j == last)
    def _():
        o_ref[...] = acc_ref[...].astype(o_ref.dtype)


def _expert_call(blk_e, blk_valid, xs, w_gate, w_up, w_down, f_tiles):
    n_rows, d = xs.shape
    f = w_gate.shape[2]
    tf = f // f_tiles
    n_blk = n_rows // MOE_TILE
    grid_spec = pltpu.PrefetchScalarGridSpec(
        num_scalar_prefetch=2,
        grid=(n_blk, f_tiles),
        in_specs=[pl.BlockSpec((MOE_TILE, d), lambda i, j, be, bv: (i, 0)),
                  pl.BlockSpec((None, d, tf), lambda i, j, be, bv: (be[i], 0, j)),
                  pl.BlockSpec((None, d, tf), lambda i, j, be, bv: (be[i], 0, j)),
                  pl.BlockSpec((None, tf, d), lambda i, j, be, bv: (be[i], j, 0))],
        out_specs=pl.BlockSpec((MOE_TILE, d), lambda i, j, be, bv: (i, 0)),
        scratch_shapes=[pltpu.VMEM((MOE_TILE, d), F32)],
    )
    return pl.pallas_call(
        _expert_kernel,
        out_shape=jax.ShapeDtypeStruct((n_rows, d), BF16),
        grid_spec=grid_spec,
        compiler_params=_cparams(2),
        name="moe_experts",
    )(blk_e, blk_valid, xs, w_gate, w_up, w_down)


def _combine_kernel(apply_norm, h_ref, y0_ref, y1_ref, g0_ref, g1_ref, nw_ref, o_ref):
    out = h_ref[...] + (g0_ref[...] * y0_ref[...].astype(F32) + g1_ref[...] * y1_ref[...].astype(F32))
    if apply_norm:
        out = _rms(out, nw_ref[...])
    o_ref[...] = out


def _combine_call(h2, y01, g0, g1, norm_w, apply_norm):
    n_tok, d = h2.shape
    tile = min(ROW_TILE, n_tok)
    n_tiles = n_tok // tile
    row = pl.BlockSpec((tile, d), lambda i: (i, 0))
    row1 = pl.BlockSpec((tile, d), lambda i: (i + n_tiles, 0))
    col = pl.BlockSpec((tile, 1), lambda i: (i, 0))
    return pl.pallas_call(
        functools.partial(_combine_kernel, apply_norm),
        out_shape=jax.ShapeDtypeStruct(h2.shape, F32),
        grid=(n_tiles,),
        in_specs=[row, row, row1, col, col, pl.BlockSpec(norm_w.shape, lambda i: (0, 0))],
        out_specs=row,
        compiler_params=_cparams(1),
        name="moe_combine",
    )(h2, y01, y01, g0, g1, norm_w)


def _final_norm_kernel(h_ref, nw_ref, o_ref):
    o_ref[...] = _rms(h_ref[...], nw_ref[...])


def _final_norm_call(h2, norm_w):
    n_tok, d = h2.shape
    tile = min(ROW_TILE, n_tok)
    row = pl.BlockSpec((tile, d), lambda i: (i, 0))
    return pl.pallas_call(
        _final_norm_kernel,
        out_shape=jax.ShapeDtypeStruct(h2.shape, F32),
        grid=(n_tok // tile,),
        in_specs=[row, pl.BlockSpec(norm_w.shape, lambda i: (0, 0))],
        out_specs=row,
        compiler_params=_cparams(1),
        name="final_norm",
    )(h2, norm_w)


def _moe(h2, norm_w, router, w_gate, w_up, w_down, final_w, apply_final):
    n_tok, d = h2.shape
    router_pad = jnp.zeros((d, LANE), F32).at[:, :N_EXPERTS].set(router)
    hn, idx, gate = _route_call(h2, norm_w, jnp.stack(_split2(router_pad)))
    e_flat = idx[:TOP_K].reshape(-1)
    onehot = (e_flat[:, None] == jnp.arange(N_EXPERTS, dtype=jnp.int32)[None, :]).astype(jnp.int32)
    csum = jnp.cumsum(onehot, axis=0)
    rank = jnp.sum(csum * onehot, axis=1) - 1
    counts = csum[-1]
    padded = (counts + MOE_TILE - 1) // MOE_TILE * MOE_TILE
    pad_end = jnp.cumsum(padded)
    pad_start = pad_end - padded
    dest = pad_start[e_flat] + rank
    n_assign = n_tok * TOP_K
    n_blk = -(-n_assign // MOE_TILE) + N_EXPERTS
    blk_start = jnp.arange(n_blk, dtype=jnp.int32) * MOE_TILE
    blk_e = jnp.minimum(jnp.searchsorted(pad_end, blk_start, side="right"), N_EXPERTS - 1).astype(jnp.int32)
    blk_valid = (blk_start < pad_end[-1]).astype(jnp.int32)
    order = jnp.argsort(e_flat, stable=True).astype(jnp.int32)
    tok_sorted = jnp.concatenate([order % n_tok, jnp.zeros((n_assign,), jnp.int32)])
    grp_start = jnp.cumsum(counts) - counts
    buf = jnp.zeros((n_blk * MOE_TILE + n_assign,), jnp.int32)
    for e in range(N_EXPERTS):
        run = lax.dynamic_slice(tok_sorted, (grp_start[e],), (n_assign,))
        buf = lax.dynamic_update_slice(buf, run, (pad_start[e],))
    src_tok = buf[:n_blk * MOE_TILE]
    xs = hn.at[src_tok].get(mode="promise_in_bounds")
    f_tiles = 2 if w_gate.shape[2] % (2 * LANE) == 0 else 1
    ys = _expert_call(blk_e, blk_valid, xs, w_gate, w_up, w_down, f_tiles)
    y01 = ys.at[dest].get(mode="promise_in_bounds")
    g0 = gate[0].reshape(n_tok, 1)
    g1 = gate[1].reshape(n_tok, 1)
    return _combine_call(h2, y01, g0, g1, final_w, apply_final)


def _pad_cols(w, n):
    return jnp.concatenate([w, jnp.zeros((w.shape[0], n - w.shape[1]), w.dtype)], axis=1)


def _token_mix(h, norm_w, w_in, lb, hgrn_norm, s5, s5_d, s5_glu_w, s5_glu_b, dn_conv_w, dn_a_log, dn_dt_bias,
               dn_norm, gla_w_lr, gla_b_lr, gla_norm, w_branch, w_out, rs, tril):
    bsz, seq, d = h.shape
    m = MIX_W
    o = 0
    w_a = w_in[:, o:o + 4 * m]; o += 4 * m
    w_s5 = w_in[:, o:o + m]; o += m
    w_qkv = w_in[:, o:o + 3 * m]; o += 3 * m
    w_bd = w_in[:, o:o + 2 * DN_HEADS]; o += 2 * DN_HEADS
    w_dg = w_in[:, o:o + m]; o += m
    nqk = GLA_HEADS * GLA_DK
    w_gqk = w_in[:, o:o + 2 * nqk]; o += 2 * nqk
    w_gv = w_in[:, o:o + m]; o += m
    w_glr = w_in[:, o:o + GLA_RANK]; o += GLA_RANK
    w_gg = w_in[:, o:o + m]; o += m
    w_m = w_in[:, o:o + N_BRANCH * d]

    w_a = w_a.astype(BF16)
    w_c = jnp.concatenate([w_qkv, w_dg, _pad_cols(w_bd, LANE)], axis=1).astype(BF16)
    w_d = jnp.concatenate([w_gqk, w_gv, w_gg, _pad_cols(w_glr, LANE)], axis=1).astype(BF16)
    row = lambda a: a.reshape(1, -1).astype(F32)
    nw = row(norm_w)

    dummy = jnp.zeros((1, LANE), F32)
    y_a = _gla_call("hgrn", h, nw, w_a, row(lb), dummy, row(hgrn_norm), rs)

    wlr_pad = jnp.zeros((LANE, nqk), F32).at[:GLA_RANK].set(gla_w_lr)
    y_d = _gla_call("gla", h, nw, w_d, wlr_pad, row(gla_b_lr), row(gla_norm), rs)

    lane_row = lambda a: jnp.zeros((1, LANE), F32).at[0, DN_HEADS:2 * DN_HEADS].set(a)
    y_c = _delta_call(h, nw, w_c, dn_conv_w.astype(F32), lane_row(dn_a_log), lane_row(dn_dt_bias), row(dn_norm), tril)

    bre, bim, ar, ai, cre, cim = s5
    y_b = _s5_call(h, nw, w_s5.astype(BF16), bre, bim, ar, ai, cre, cim, row(s5_d), s5_glu_w.astype(BF16),
                   row(s5_glu_b))

    return _merge_call(h, nw, w_m.astype(BF16), y_a, y_b, y_c, y_d, w_branch.astype(BF16), w_out.astype(BF16))


def kernel(x, attn_norm, ffn_norm, final_norm, w_in, hgrn_lb, hgrn_norm, s5_a_re, s5_a_im, s5_log_dt, s5_b_re, s5_b_im, s5_c_re, s5_c_im, s5_d, s5_glu_w, s5_glu_b, dn_conv_w, dn_a_log, dn_dt_bias, dn_norm, gla_w_lr, gla_b_lr, gla_norm, w_branch, w_out, ffn_w_gate, ffn_w_up, ffn_w_down, moe_router, moe_w_gate, moe_w_up, moe_w_down):
    bsz, seq, d = x.shape
    depth = w_in.shape[0]
    assert bsz == SUBLANE, "the S5 kernel keeps one batch row per sublane"
    assert seq % CHUNK == 0 and d == D_MODEL
    lb_soft = jax.nn.softmax(hgrn_lb.astype(F32), axis=0)
    lb_all = jnp.cumsum(lb_soft, axis=0) - lb_soft[0]
    rs = jnp.asarray(_range_sum_matrix(), BF16)
    tril = jnp.asarray(np.tile(np.tril(np.ones((CHUNK, CHUNK), np.float32)), (1, 3)), BF16)
    row = lambda a: a.reshape(1, -1).astype(F32)

    h = x
    for layer in range(depth):
        s5 = _s5_params(s5_a_re[layer], s5_a_im[layer], s5_log_dt[layer], s5_b_re[layer], s5_b_im[layer],
                        s5_c_re[layer], s5_c_im[layer], bsz)
        h = _token_mix(h, attn_norm[layer], w_in[layer], lb_all[layer], hgrn_norm[layer], s5, s5_d[layer],
                       s5_glu_w[layer], s5_glu_b[layer], dn_conv_w[layer], dn_a_log[layer], dn_dt_bias[layer],
                       dn_norm[layer], gla_w_lr[layer], gla_b_lr[layer], gla_norm[layer], w_branch[layer],
                       w_out[layer], rs, tril)
        h2 = h.reshape(bsz * seq, d)
        j = layer // 2
        last = layer == depth - 1
        if layer % 2 == 0:
            f = ffn_w_gate.shape[2]
            f_tiles = 2 if f % (2 * LANE) == 0 else 1
            h2 = _ffn_call(h2, row(ffn_norm[layer]), ffn_w_gate[j].astype(BF16), ffn_w_up[j].astype(BF16),
                           ffn_w_down[j].astype(BF16), f_tiles)
            if last:
                h2 = _final_norm_call(h2, row(final_norm))
        else:
            h2 = _moe(h2, row(ffn_norm[layer]), moe_router[j], moe_w_gate[j].astype(BF16),
                      moe_w_up[j].astype(BF16), moe_w_down[j].astype(BF16), row(final_norm), last)
        h = h2.reshape(bsz, seq, d)
    return h
```

```python
import functools

import numpy as np
import jax
import jax.numpy as jnp
from jax import lax
from jax.experimental import pallas as pl
from jax.experimental.pallas import tpu as pltpu

F32 = jnp.float32
BF16 = jnp.bfloat16
HIGHEST = lax.Precision.HIGHEST

D_MODEL = 1024
MIX_W = 512
HG_HEADS = 4
HG_DH = MIX_W // HG_HEADS
S5_GROUP = 16
S5_GROUPS = MIX_W // S5_GROUP
S5_STATE = 64
DN_HEADS = 4
DN_DH = MIX_W // DN_HEADS
CONV_K = 4
GLA_HEADS = 4
GLA_DK = 64
GLA_DV = MIX_W // GLA_HEADS
GLA_RANK = 16
GLA_NORMALIZER = 16.0
CHUNK = 64
N_BRANCH = 4
N_EXPERTS = 8
TOP_K = 2
NORM_EPS = 1e-6

LANE = 128
SUBLANE = 8
VMEM_LIMIT = 56 * 1024 * 1024

MIX_TILE = 512
S5_STEPS = 128
S5_CLUSTER = 8
ROW_TILE = 512
MOE_TILE = 512
LEVELS = (32, 16, 8, 4, 2, 1)


def _cparams(n_axes):
    return pltpu.CompilerParams(dimension_semantics=("arbitrary",) * n_axes, vmem_limit_bytes=VMEM_LIMIT)


def _rms(x, w):
    return x * lax.rsqrt(jnp.mean(x * x, axis=-1, keepdims=True) + NORM_EPS) * w


def _dot(a, b):
    return jnp.dot(a, b, preferred_element_type=F32)


def _dot_nt(a, b):
    return lax.dot_general(a, b, (((1,), (1,)), ((), ())), preferred_element_type=F32)


def _dot_tn(a, b):
    return lax.dot_general(a, b, (((0,), (0,)), ((), ())), preferred_element_type=F32)


def _split3(x):
    hi = x.astype(BF16)
    r1 = x - hi.astype(F32)
    mid = r1.astype(BF16)
    lo = (r1 - mid.astype(F32)).astype(BF16)
    return hi, mid, lo


def _split2(x):
    hi = x.astype(BF16)
    return hi, (x - hi.astype(F32)).astype(BF16)


def _lhs3(hi, lo):
    return jnp.concatenate([hi, hi, lo], axis=1)


def _rhs3(hi, lo):
    return jnp.concatenate([hi, lo, hi], axis=0)


def _mm3(lhs3, rhs3):
    return jnp.dot(lhs3, rhs3, preferred_element_type=F32)


def _range_sum_matrix():
    c = CHUNK
    m = np.zeros((2 + len(LEVELS), c, c), np.float32)
    for r in range(c):
        m[0, r, :r + 1] = 1.0
        m[1, r, r + 1:] = 1.0
        for l, b in enumerate(LEVELS):
            mid = (r // (2 * b)) * 2 * b + b - 1
            if r > mid:
                m[2 + l, r, mid + 1:r + 1] = 1.0
            else:
                m[2 + l, r, r + 1:mid + 1] = 1.0
    m = m.reshape(-1, c)
    return np.concatenate([m, m, m], axis=1)


def _head_norm_gate(o, w, gate):
    return o * lax.rsqrt(jnp.mean(o * o, axis=-1, keepdims=True) + NORM_EPS) * w * jax.nn.silu(gate)


def _gla_chunks(n_heads, dk, dv, q_ref, k_ref, v_ref, gs_ref, rs_ref, o_ref, st_ref, n_chunks):
    ii = lax.broadcasted_iota(jnp.int32, (CHUNK, CHUNK), 0)
    jj = lax.broadcasted_iota(jnp.int32, (CHUNK, CHUNK), 1)
    masks = [((ii // (2 * b)) == (jj // (2 * b))) & ((ii % (2 * b)) >= b) & ((jj % (2 * b)) < b) for b in LEVELS]
    chunks = range(n_chunks)
    heads = range(n_heads)
    rows = [slice(c * CHUNK, (c + 1) * CHUNK) for c in chunks]

    xs = []
    for c in chunks:
        g3 = jnp.concatenate([gs_ref[0, rows[c], :], gs_ref[1, rows[c], :], gs_ref[2, rows[c], :]], axis=0)
        xs.append(jnp.exp(_dot(rs_ref[...], g3)))

    scores, kv, qds, decs, diags = {}, {}, {}, {}, {}
    for c in chunks:
        qc, kc, vc = q_ref[rows[c], :], k_ref[rows[c], :], v_ref[rows[c], :]
        x = xs[c]
        for h in heads:
            cols = slice(h * dk, (h + 1) * dk)
            qh, kh = qc[:, cols], kc[:, cols]
            vh = vc[:, h * dv:(h + 1) * dv]
            a = jnp.zeros((CHUNK, CHUNK), F32)
            for l in range(len(LEVELS)):
                xl = x[(2 + l) * CHUNK:(3 + l) * CHUNK, cols]
                al = _dot_nt((qh * xl).astype(BF16), (kh * xl).astype(BF16))
                a = jnp.where(masks[l], al, a)
            scores[c, h] = a.astype(BF16)
            kv[c, h] = _dot_tn(vh.astype(BF16), (kh * x[CHUNK:2 * CHUNK, cols]).astype(BF16))
            qds[c, h] = (qh * x[0:CHUNK, cols]).astype(BF16)
            decs[c, h] = x[CHUNK - 1:CHUNK, cols]
            diags[c, h] = jnp.sum(qh * kh, axis=-1, keepdims=True) * vh

    for c in chunks:
        vc = v_ref[rows[c], :]
        for h in heads:
            vb = vc[:, h * dv:(h + 1) * dv].astype(BF16)
            o_ref[rows[c], h * dv:(h + 1) * dv] = _dot(scores[c, h], vb) + diags[c, h]

    for h in heads:
        st = st_ref[h]
        for c in chunks:
            o_ref[rows[c], h * dv:(h + 1) * dv] += _dot_nt(qds[c, h], st.astype(BF16))
            st = st * decs[c, h] + kv[c, h]
        st_ref[h] = st


def _gla_kernel(mode, h_ref, nw_ref, w_ref, aux1_ref, aux2_ref, hw_ref, rs_ref, y_ref,
                p_ref, q_ref, k_ref, gs_ref, o_ref, st_ref):
    if mode == "hgrn":
        n_heads, dk, dv = HG_HEADS, HG_DH, HG_DH
    else:
        n_heads, dk, dv = GLA_HEADS, GLA_DK, GLA_DV
    n = n_heads * dk
    tile = h_ref.shape[0]

    @pl.when(pl.program_id(1) == 0)
    def _():
        st_ref[...] = jnp.zeros_like(st_ref)

    xn = _rms(h_ref[...], nw_ref[...]).astype(BF16)
    p_ref[...] = _dot(xn, w_ref[...])
    if mode == "hgrn":
        lb = aux1_ref[...]
        z = p_ref[:, n:2 * n]
        g = jnp.log(lb + (1.0 - lb) * jax.nn.sigmoid(z))
        q_ref[...] = p_ref[:, 0:n] * dk ** -0.5
        k_ref[...] = (1.0 - lb) * jax.nn.sigmoid(-z)
        v_off = 2 * n
    else:
        v_off = 2 * n
        glr = p_ref[:, v_off + 2 * MIX_W:v_off + 2 * MIX_W + LANE]
        logit = jnp.dot(glr, aux1_ref[...], preferred_element_type=F32, precision=HIGHEST) + aux2_ref[...]
        g = jax.nn.log_sigmoid(logit) / GLA_NORMALIZER
        q_ref[...] = p_ref[:, 0:n] * dk ** -0.5
        k_ref[...] = p_ref[:, n:2 * n]
    hi, mid, lo = _split3(g)
    gs_ref[0] = hi
    gs_ref[1] = mid
    gs_ref[2] = lo

    _gla_chunks(n_heads, dk, dv, q_ref, k_ref, p_ref.at[:, v_off:v_off + MIX_W], gs_ref, rs_ref, o_ref, st_ref,
                tile // CHUNK)

    g_off = v_off + MIX_W
    for h in range(n_heads):
        cols = slice(h * dv, (h + 1) * dv)
        y = _head_norm_gate(o_ref[:, cols], hw_ref[...], p_ref[:, g_off + h * dv:g_off + (h + 1) * dv])
        y_ref[:, cols] = y.astype(y_ref.dtype)


def _gla_call(mode, h, norm_w, w, aux1, aux2, head_w, rs):
    bsz, seq, d = h.shape
    n_heads, dk, dv = (HG_HEADS, HG_DH, HG_DH) if mode == "hgrn" else (GLA_HEADS, GLA_DK, GLA_DV)
    n = n_heads * dk
    tile = min(MIX_TILE, seq)
    ncols = w.shape[1]
    full = lambda a: pl.BlockSpec(a.shape, lambda b, t: (0,) * a.ndim)
    return pl.pallas_call(
        functools.partial(_gla_kernel, mode),
        out_shape=jax.ShapeDtypeStruct((bsz, seq, MIX_W), BF16),
        grid=(bsz, seq // tile),
        in_specs=[pl.BlockSpec((None, tile, d), lambda b, t: (b, t, 0)),
                  full(norm_w), full(w), full(aux1), full(aux2), full(head_w), full(rs)],
        out_specs=pl.BlockSpec((None, tile, MIX_W), lambda b, t: (b, t, 0)),
        scratch_shapes=[pltpu.VMEM((tile, ncols), F32), pltpu.VMEM((tile, n), F32), pltpu.VMEM((tile, n), F32),
                        pltpu.VMEM((3, tile, n), BF16), pltpu.VMEM((tile, MIX_W), F32),
                        pltpu.VMEM((n_heads, dv, dk), F32)],
        compiler_params=_cparams(2),
        name="mix_" + mode,
    )(h, norm_w, w, aux1, aux2, head_w, rs)


def _delta_kernel(h_ref, nw_ref, w_ref, cw_ref, alog_ref, dtb_ref, hw_ref, tril_ref, y_ref,
                  p_ref, xs_ref, q_ref, k_ref, v_ref, sm_ref, o_ref, st_ref):
    tile = h_ref.shape[0]
    nh, dh = DN_HEADS, DN_DH
    qkv = 3 * MIX_W

    @pl.when(pl.program_id(1) == 0)
    def _():
        st_ref[...] = jnp.zeros_like(st_ref)
        xs_ref[0:SUBLANE, :] = jnp.zeros((SUBLANE, qkv), F32)

    xn = _rms(h_ref[...], nw_ref[...]).astype(BF16)
    p_ref[...] = _dot(xn, w_ref[...])

    xs_ref[SUBLANE:SUBLANE + tile, :] = p_ref[:, 0:qkv]
    conv = cw_ref[CONV_K - 1:CONV_K, :] * xs_ref[SUBLANE:SUBLANE + tile, :]
    for j in range(1, CONV_K):
        conv = conv + cw_ref[CONV_K - 1 - j:CONV_K - j, :] * xs_ref[SUBLANE - j:SUBLANE - j + tile, :]
    xs_ref[0:SUBLANE, :] = xs_ref[tile:tile + SUBLANE, :]
    conv = jax.nn.silu(conv)
    for h in range(nh):
        cols = slice(h * dh, (h + 1) * dh)
        qh = conv[:, h * dh:(h + 1) * dh]
        kh = conv[:, MIX_W + h * dh:MIX_W + (h + 1) * dh]
        q_ref[:, cols] = qh * lax.rsqrt(jnp.sum(qh * qh, axis=-1, keepdims=True) + NORM_EPS) * dh ** -0.5
        k_ref[:, cols] = kh * lax.rsqrt(jnp.sum(kh * kh, axis=-1, keepdims=True) + NORM_EPS)
    v_ref[...] = conv[:, 2 * MIX_W:3 * MIX_W]

    small = p_ref[:, qkv + MIX_W:qkv + MIX_W + LANE]
    gdec = -jnp.exp(alog_ref[...]) * jax.nn.softplus(small + dtb_ref[...])
    hi, mid, lo = _split3(gdec)
    sm_ref[0] = hi
    sm_ref[1] = mid
    sm_ref[2] = lo

    pw_ = nh * CHUNK
    ii = lax.broadcasted_iota(jnp.int32, (CHUNK, pw_), 0)
    lane_p = lax.broadcasted_iota(jnp.int32, (CHUNK, pw_), 1)
    jj = lane_p % CHUNK
    incl = ii >= jj
    strict = ii > jj
    eye = (ii == jj).astype(F32)
    head_p = (lane_p // CHUNK).astype(F32).astype(BF16)
    head_w = (lax.broadcasted_iota(jnp.int32, (CHUNK, MIX_W), 1) // dh).astype(F32).astype(BF16)
    sel_p = [head_p == h for h in range(nh)]
    sel_w = [head_w == h for h in range(nh)]
    n_chunks = tile // CHUNK

    def block_diag(y, sel):
        return jnp.concatenate([jnp.where(sel[h], y, jnp.zeros_like(y)) for h in range(nh)], axis=0)

    def head_cols(x, lo, width):
        return jnp.concatenate([jnp.broadcast_to(x[:, lo + h:lo + h + 1], (x.shape[0], width)) for h in range(nh)],
                               axis=1)

    gcums = []
    for c in range(n_chunks):
        rows = slice(c * CHUNK, (c + 1) * CHUNK)
        g3 = jnp.concatenate([sm_ref[0, rows, :], sm_ref[1, rows, :], sm_ref[2, rows, :]], axis=0)
        gcums.append(_dot(tril_ref[...], g3))
    prob = []
    for c in range(n_chunks):
        rows = slice(c * CHUNK, (c + 1) * CHUNK)
        gcum = gcums[c]
        gcum_t = gcum.T
        beta_c = jax.nn.sigmoid(p_ref[rows, qkv + MIX_W:qkv + MIX_W + LANE])
        qc, kc, vc = q_ref[rows, :], k_ref[rows, :], v_ref[rows, :]
        gcol_p = head_cols(gcum, nh, CHUNK)
        grow_p = jnp.concatenate([gcum_t[nh + h:nh + h + 1, :] for h in range(nh)], axis=1)
        decay = jnp.exp(jnp.where(incl, gcol_p - grow_p, -jnp.inf))
        gcol_w = head_cols(gcum, nh, dh)
        glast_w = gcol_w[CHUNK - 1:CHUNK, :]
        beta_w = head_cols(beta_c, 0, dh)
        egc = jnp.exp(gcol_w)
        kb = kc * beta_w
        k_bd = block_diag(kc.astype(BF16), sel_w)
        kk_qk = _dot_nt(jnp.concatenate([kb.astype(BF16), qc.astype(BF16)], axis=0), k_bd)
        lower = jnp.where(strict, kk_qk[0:CHUNK] * decay, 0.0)
        prob.append(dict(
            lower=lower, vb_bd=block_diag((vc * beta_w).astype(BF16), sel_w),
            kbg_bd=block_diag((kb * egc).astype(BF16), sel_w),
            qk=(kk_qk[CHUNK:2 * CHUNK] * decay).astype(BF16), q_dec=(qc * egc).astype(BF16),
            k_dec=(kc * jnp.exp(glast_w - gcol_w)).astype(BF16), a=jnp.exp(glast_w)))

    def rhs3_bd(hi, lo):
        bh, bl = block_diag(hi, sel_p), block_diag(lo, sel_p)
        return jnp.concatenate([bh, bl, bh], axis=0)

    t_inv = [eye - pr["lower"] for pr in prob]
    pw = []
    for pr in prob:
        hi, lo = _split2(pr["lower"])
        pw.append(_mm3(_lhs3(hi, lo), rhs3_bd(hi, lo)))
    for it in range(5):
        for i in range(n_chunks):
            hi, lo = _split2(pw[i])
            lhs_t = _lhs3(*_split2(t_inv[i]))
            if it < 4:
                both = _mm3(jnp.concatenate([lhs_t, _lhs3(hi, lo)], axis=0), rhs3_bd(hi, lo))
                t_inv[i] = t_inv[i] + both[0:CHUNK]
                pw[i] = both[CHUNK:2 * CHUNK]
            else:
                t_inv[i] = t_inv[i] + _mm3(lhs_t, rhs3_bd(hi, lo))
    for pr, t in zip(prob, t_inv):
        tb = t.astype(BF16)
        pr["u"] = _dot(tb, pr["vb_bd"])
        pr["w"] = _dot(tb, pr["kbg_bd"]).astype(BF16)

    st = [st_ref[h] for h in range(nh)]
    for c in range(n_chunks):
        rows = slice(c * CHUNK, (c + 1) * CHUNK)
        pr = prob[c]
        hcols = [slice(h * dh, (h + 1) * dh) for h in range(nh)]
        sb = [st[h].astype(BF16) for h in range(nh)]
        ws = [_dot(pr["w"][:, hcols[h]], sb[h]) for h in range(nh)]
        qs = [_dot(pr["q_dec"][:, hcols[h]], sb[h]) for h in range(nh)]
        vnb = [(pr["u"][:, hcols[h]] - ws[h]).astype(BF16) for h in range(nh)]
        for h in range(nh):
            o_ref[rows, hcols[h]] = qs[h] + _dot(pr["qk"][:, h * CHUNK:(h + 1) * CHUNK], vnb[h])
        for h in range(nh):
            st[h] = st[h] * pr["a"][:, hcols[h]] + _dot_tn(pr["k_dec"][:, hcols[h]], vnb[h])
    for h in range(nh):
        st_ref[h] = st[h]

    g_off = qkv
    for h in range(nh):
        cols = slice(h * dh, (h + 1) * dh)
        y = _head_norm_gate(o_ref[:, cols], hw_ref[...], p_ref[:, g_off + h * dh:g_off + (h + 1) * dh])
        y_ref[:, cols] = y.astype(y_ref.dtype)


def _delta_call(h, norm_w, w, conv_w, alog_row, dtb_row, head_w, tril):
    bsz, seq, d = h.shape
    tile = min(MIX_TILE, seq)
    ncols = w.shape[1]
    full = lambda a: pl.BlockSpec(a.shape, lambda b, t: (0,) * a.ndim)
    return pl.pallas_call(
        _delta_kernel,
        out_shape=jax.ShapeDtypeStruct((bsz, seq, MIX_W), BF16),
        grid=(bsz, seq // tile),
        in_specs=[pl.BlockSpec((None, tile, d), lambda b, t: (b, t, 0)),
                  full(norm_w), full(w), full(conv_w), full(alog_row), full(dtb_row), full(head_w), full(tril)],
        out_specs=pl.BlockSpec((None, tile, MIX_W), lambda b, t: (b, t, 0)),
        scratch_shapes=[pltpu.VMEM((tile, ncols), F32), pltpu.VMEM((tile + SUBLANE, 3 * MIX_W), F32),
                        pltpu.VMEM((tile, MIX_W), F32), pltpu.VMEM((tile, MIX_W), F32),
                        pltpu.VMEM((tile, MIX_W), F32), pltpu.VMEM((3, tile, LANE), BF16),
                        pltpu.VMEM((tile, MIX_W), F32), pltpu.VMEM((DN_HEADS, DN_DH, DN_DH), F32)],
        compiler_params=_cparams(2),
        name="mix_delta",
    )(h, norm_w, w, conv_w, alog_row, dtb_row, head_w, tril)


def _s5_kernel(h_ref, nw_ref, w_ref, bre_ref, bim_ref, ar_ref, ai_ref, cre_ref, cim_ref, d_ref, gw_ref, gb_ref,
               y_ref, u_ref, xr_ref, xi_ref, sr_ref, si_ref):
    bsz, steps, d = h_ref.shape
    rows = bsz * steps
    n_cl = bre_ref.shape[0]
    cw = bre_ref.shape[1]
    sw = bre_ref.shape[2]

    @pl.when(pl.program_id(0) == 0)
    def _():
        sr_ref[...] = jnp.zeros_like(sr_ref)
        si_ref[...] = jnp.zeros_like(si_ref)

    xn = _rms(h_ref[...], nw_ref[...]).astype(BF16).reshape(rows, d)
    p = _dot(xn, w_ref[...])
    n_lt = u_ref.shape[0]
    for b in range(bsz):
        for j in range(n_lt):
            u_ref[j, pl.ds(b, steps, stride=bsz), :] = p[b * steps:(b + 1) * steps, j * LANE:(j + 1) * LANE]

    u = jnp.concatenate([u_ref[j] for j in range(n_lt)], axis=1)
    ub = u.astype(BF16)
    def input_term(c):
        uc = ub[:, c * cw:(c + 1) * cw]
        xr_ref[:, c * sw:(c + 1) * sw] = _dot(uc, bre_ref[c])
        xi_ref[:, c * sw:(c + 1) * sw] = _dot(uc, bim_ref[c])

    def scan(c):
        ls = slice(c * sw, (c + 1) * sw)
        ar, ai = ar_ref[:, ls], ai_ref[:, ls]
        xr, xi = sr_ref[:, ls], si_ref[:, ls]
        for i in range(steps):
            rr = slice(i * bsz, (i + 1) * bsz)
            xr, xi = ar * xr - ai * xi + xr_ref[rr, ls], ar * xi + ai * xr + xi_ref[rr, ls]
            xr_ref[rr, ls] = xr
            xi_ref[rr, ls] = xi
        sr_ref[:, ls] = xr
        si_ref[:, ls] = xi

    def output_term(c):
        ls = slice(c * sw, (c + 1) * sw)
        return _dot(xr_ref[:, ls].astype(BF16), cre_ref[c]) - _dot(xi_ref[:, ls].astype(BF16), cim_ref[c])

    ys = [None] * n_cl
    input_term(0)
    for c in range(n_cl):
        if c + 1 < n_cl:
            input_term(c + 1)
        if c > 0:
            ys[c - 1] = output_term(c - 1)
        scan(c)
    ys[n_cl - 1] = output_term(n_cl - 1)
    y = jnp.concatenate(ys, axis=1) + d_ref[...] * u
    z = jax.nn.gelu(y)
    gl = _dot(z.astype(BF16), gw_ref[...]) + gb_ref[...]
    out = z * jax.nn.sigmoid(gl)
    for j in range(n_lt):
        u_ref[j] = out[:, j * LANE:(j + 1) * LANE]
    for b in range(bsz):
        for j in range(n_lt):
            y_ref[b, :, j * LANE:(j + 1) * LANE] = u_ref[j, pl.ds(b, steps, stride=bsz), :].astype(y_ref.dtype)


def _s5_call(h, norm_w, w, bre, bim, ar, ai, cre, cim, d_row, glu_w, glu_b):
    bsz, seq, d = h.shape
    steps = min(S5_STEPS, seq)
    rows = steps * bsz
    n_state = ar.shape[1]
    full = lambda a: pl.BlockSpec(a.shape, lambda t: (0,) * a.ndim)
    return pl.pallas_call(
        _s5_kernel,
        out_shape=jax.ShapeDtypeStruct((bsz, seq, MIX_W), BF16),
        grid=(seq // steps,),
        in_specs=[pl.BlockSpec((bsz, steps, d), lambda t: (0, t, 0)), full(norm_w), full(w),
                  full(bre), full(bim), full(ar), full(ai), full(cre), full(cim), full(d_row), full(glu_w),
                  full(glu_b)],
        out_specs=pl.BlockSpec((bsz, steps, MIX_W), lambda t: (0, t, 0)),
        scratch_shapes=[pltpu.VMEM((MIX_W // LANE, rows, LANE), F32),
                        pltpu.VMEM((rows, n_state), F32), pltpu.VMEM((rows, n_state), F32),
                        pltpu.VMEM((bsz, n_state), F32), pltpu.VMEM((bsz, n_state), F32)],
        compiler_params=_cparams(1),
        name="mix_s5",
    )(h, norm_w, w, bre, bim, ar, ai, cre, cim, d_row, glu_w, glu_b)


def _s5_params(a_re, a_im, log_dt, b_re, b_im, c_re, c_im, bsz):
    dt = jnp.exp(log_dt)[:, None]
    mag = jnp.exp(a_re * dt)
    abar_r, abar_i = mag * jnp.cos(a_im * dt), mag * jnp.sin(a_im * dt)
    den = a_re * a_re + a_im * a_im
    nr, ni = abar_r - 1.0, abar_i
    zr, zi = (nr * a_re + ni * a_im) / den, (ni * a_re - nr * a_im) / den
    bbar_r = zr[..., None] * b_re - zi[..., None] * b_im
    bbar_i = zr[..., None] * b_im + zi[..., None] * b_re
    n_cl = S5_GROUPS // S5_CLUSTER
    eye = jnp.eye(S5_CLUSTER, dtype=F32)

    def pack_b(bb):
        bb = bb.reshape(n_cl, S5_CLUSTER, S5_STATE, S5_GROUP)
        out = jnp.einsum("cgph,gk->cghkp", bb, eye)
        return out.reshape(n_cl, S5_CLUSTER * S5_GROUP, S5_CLUSTER * S5_STATE).astype(BF16)

    def pack_c(cc):
        cc = cc.reshape(n_cl, S5_CLUSTER, S5_GROUP, S5_STATE)
        out = jnp.einsum("cghp,gk->cgpkh", cc, eye)
        return out.reshape(n_cl, S5_CLUSTER * S5_STATE, S5_CLUSTER * S5_GROUP).astype(BF16)

    bc = lambda a: jnp.broadcast_to(a.reshape(1, -1), (bsz, S5_GROUPS * S5_STATE))
    return pack_b(bbar_r), pack_b(bbar_i), bc(abar_r), bc(abar_i), pack_c(c_re), pack_c(c_im)


def _merge_kernel(h_ref, nw_ref, wm_ref, ya_ref, yb_ref, yc_ref, yd_ref, wb_ref, wo_ref, o_ref):
    h = h_ref[...]
    xn = _rms(h, nw_ref[...]).astype(BF16)
    d = h.shape[1]
    sub = 2 * LANE
    ys = [y_ref[...] for y_ref in (ya_ref, yb_ref, yc_ref, yd_ref)]
    parts = []
    for c in range(d // sub):
        mixed = jnp.zeros((h.shape[0], sub), F32)
        for n in range(N_BRANCH):
            gate = jax.nn.sigmoid(_dot(xn, wm_ref[:, n * d + c * sub:n * d + (c + 1) * sub]))
            mixed = mixed + gate * _dot(ys[n], wb_ref[n, :, c * sub:(c + 1) * sub])
        parts.append(mixed.astype(BF16))
    o_ref[...] = h + _dot(jnp.concatenate(parts, axis=1), wo_ref[...])


def _merge_call(h, norm_w, w_m, y_a, y_b, y_c, y_d, w_b, w_o):
    bsz, seq, d = h.shape
    tile = min(ROW_TILE, seq)
    full = lambda a: pl.BlockSpec(a.shape, lambda b, t: (0,) * a.ndim)
    ytile = pl.BlockSpec((None, tile, MIX_W), lambda b, t: (b, t, 0))
    return pl.pallas_call(
        _merge_kernel,
        out_shape=jax.ShapeDtypeStruct(h.shape, F32),
        grid=(bsz, seq // tile),
        in_specs=[pl.BlockSpec((None, tile, d), lambda b, t: (b, t, 0)), full(norm_w), full(w_m),
                  ytile, ytile, ytile, ytile, full(w_b), full(w_o)],
        out_specs=pl.BlockSpec((None, tile, d), lambda b, t: (b, t, 0)),
        compiler_params=_cparams(2),
        name="merge",
    )(h, norm_w, w_m, y_a, y_b, y_c, y_d, w_b, w_o)


def _ffn_kernel(f_tiles, h_ref, nw_ref, wg_ref, wu_ref, wd_ref, o_ref):
    h = h_ref[...]
    hn = _rms(h, nw_ref[...]).astype(BF16)
    tf = wg_ref.shape[1] // f_tiles
    out = h
    for j in range(f_tiles):
        cols = slice(j * tf, (j + 1) * tf)
        act = (jax.nn.silu(_dot(hn, wg_ref[:, cols])) * _dot(hn, wu_ref[:, cols])).astype(BF16)
        out = out + _dot(act, wd_ref[cols, :])
    o_ref[...] = out


def _ffn_call(h2, norm_w, w_gate, w_up, w_down, f_tiles):
    n_tok, d = h2.shape
    tile = min(ROW_TILE, n_tok)
    resident = lambda a: pl.BlockSpec(a.shape, lambda i: (0, 0), pipeline_mode=pl.Buffered(1))
    return pl.pallas_call(
        functools.partial(_ffn_kernel, f_tiles),
        out_shape=jax.ShapeDtypeStruct(h2.shape, F32),
        grid=(n_tok // tile,),
        in_specs=[pl.BlockSpec((tile, d), lambda i: (i, 0)),
                  pl.BlockSpec(norm_w.shape, lambda i: (0, 0)),
                  resident(w_gate), resident(w_up), resident(w_down)],
        out_specs=pl.BlockSpec((tile, d), lambda i: (i, 0)),
        compiler_params=_cparams(1),
        name="ffn_dense",
    )(h2, norm_w, w_gate, w_up, w_down)


def _route_kernel(h_ref, nw_ref, rt_ref, hn_ref, idx_ref, gate_ref):
    hn = _rms(h_ref[...], nw_ref[...])
    hn_ref[...] = hn.astype(BF16)
    hn_hi, hn_lo = _split2(hn)
    logits = (_dot(hn_hi, rt_ref[0]) + _dot(hn_hi, rt_ref[1])) + _dot(hn_lo, rt_ref[0])
    lg = logits.T[0:N_EXPERTS, :]
    row = lax.broadcasted_iota(jnp.int32, lg.shape, 0)
    m1 = jnp.max(lg, axis=0, keepdims=True)
    i1 = jnp.min(jnp.where(lg == m1, row, N_EXPERTS), axis=0, keepdims=True)
    lg2 = jnp.where(row == i1, -jnp.inf, lg)
    m2 = jnp.max(lg2, axis=0, keepdims=True)
    i2 = jnp.min(jnp.where(lg2 == m2, row, N_EXPERTS), axis=0, keepdims=True)
    e2 = jnp.exp(m2 - m1)
    g1 = 1.0 / (1.0 + e2)
    g2 = e2 / (1.0 + e2)
    idx_ref[...] = jnp.where(row == 0, i1, jnp.where(row == 1, i2, 0))
    gate_ref[...] = jnp.where(row == 0, g1, jnp.where(row == 1, g2, 0.0))


def _route_call(h2, norm_w, router_pad):
    n_tok, d = h2.shape
    tile = min(ROW_TILE, n_tok)
    return pl.pallas_call(
        _route_kernel,
        out_shape=(jax.ShapeDtypeStruct((n_tok, d), BF16),
                   jax.ShapeDtypeStruct((N_EXPERTS, n_tok), jnp.int32),
                   jax.ShapeDtypeStruct((N_EXPERTS, n_tok), F32)),
        grid=(n_tok // tile,),
        in_specs=[pl.BlockSpec((tile, d), lambda i: (i, 0)),
                  pl.BlockSpec(norm_w.shape, lambda i: (0, 0)),
                  pl.BlockSpec(router_pad.shape, lambda i: (0,) * router_pad.ndim)],
        out_specs=(pl.BlockSpec((tile, d), lambda i: (i, 0)),
                   pl.BlockSpec((N_EXPERTS, tile), lambda i: (0, i)),
                   pl.BlockSpec((N_EXPERTS, tile), lambda i: (0, i))),
        compiler_params=_cparams(1),
        name="moe_route",
    )(h2, norm_w, router_pad)


def _expert_kernel(be_ref, bv_ref, x_ref, wg_ref, wu_ref, wd_ref, o_ref, acc_ref):
    i, j = pl.program_id(0), pl.program_id(1)
    last = pl.num_programs(1) - 1

    @pl.when(j == 0)
    def _():
        acc_ref[...] = jnp.zeros_like(acc_ref)

    @pl.when(bv_ref[i] != 0)
    def _():
        x = x_ref[...]
        tf = wg_ref.shape[1]
        sub = 2 * LANE if tf % (2 * LANE) == 0 else tf
        part = jnp.zeros(acc_ref.shape, F32)
        for c in range(tf // sub):
            cols = slice(c * sub, (c + 1) * sub)
            act = (jax.nn.silu(_dot(x, wg_ref[:, cols])) * _dot(x, wu_ref[:, cols])).astype(BF16)
            part = part + _dot(act, wd_ref[cols, :])
        acc_ref[...] += part

    @pl.when(j == last)
    def _():
        o_ref[...] = acc_ref[...].astype(o_ref.dtype)


def _expert_call(blk_e, blk_valid, xs, w_gate, w_up, w_down, f_tiles):
    n_rows, d = xs.shape
    f = w_gate.shape[2]
    tf = f // f_tiles
    n_blk = n_rows // MOE_TILE
    grid_spec = pltpu.PrefetchScalarGridSpec(
        num_scalar_prefetch=2,
        grid=(n_blk, f_tiles),
        in_specs=[pl.BlockSpec((MOE_TILE, d), lambda i, j, be, bv: (i, 0)),
                  pl.BlockSpec((None, d, tf), lambda i, j, be, bv: (be[i], 0, j)),
                  pl.BlockSpec((None, d, tf), lambda i, j, be, bv: (be[i], 0, j)),
                  pl.BlockSpec((None, tf, d), lambda i, j, be, bv: (be[i], j, 0))],
        out_specs=pl.BlockSpec((MOE_TILE, d), lambda i, j, be, bv: (i, 0)),
        scratch_shapes=[pltpu.VMEM((MOE_TILE, d), F32)],
    )
    return pl.pallas_call(
        _expert_kernel,
        out_shape=jax.ShapeDtypeStruct((n_rows, d), BF16),
        grid_spec=grid_spec,
        compiler_params=_cparams(2),
        name="moe_experts",
    )(blk_e, blk_valid, xs, w_gate, w_up, w_down)


def _combine_kernel(apply_norm, h_ref, y0_ref, y1_ref, g0_ref, g1_ref, nw_ref, o_ref):
    out = h_ref[...] + (g0_ref[...] * y0_ref[...].astype(F32) + g1_ref[...] * y1_ref[...].astype(F32))
    if apply_norm:
        out = _rms(out, nw_ref[...])
    o_ref[...] = out


def _combine_call(h2, y01, g0, g1, norm_w, apply_norm):
    n_tok, d = h2.shape
    tile = min(ROW_TILE, n_tok)
    n_tiles = n_tok // tile
    row = pl.BlockSpec((tile, d), lambda i: (i, 0))
    row1 = pl.BlockSpec((tile, d), lambda i: (i + n_tiles, 0))
    col = pl.BlockSpec((tile, 1), lambda i: (i, 0))
    return pl.pallas_call(
        functools.partial(_combine_kernel, apply_norm),
        out_shape=jax.ShapeDtypeStruct(h2.shape, F32),
        grid=(n_tiles,),
        in_specs=[row, row, row1, col, col, pl.BlockSpec(norm_w.shape, lambda i: (0, 0))],
        out_specs=row,
        compiler_params=_cparams(1),
        name="moe_combine",
    )(h2, y01, y01, g0, g1, norm_w)


def _final_norm_kernel(h_ref, nw_ref, o_ref):
    o_ref[...] = _rms(h_ref[...], nw_ref[...])


def _final_norm_call(h2, norm_w):
    n_tok, d = h2.shape
    tile = min(ROW_TILE, n_tok)
    row = pl.BlockSpec((tile, d), lambda i: (i, 0))
    return pl.pallas_call(
        _final_norm_kernel,
        out_shape=jax.ShapeDtypeStruct(h2.shape, F32),
        grid=(n_tok // tile,),
        in_specs=[row, pl.BlockSpec(norm_w.shape, lambda i: (0, 0))],
        out_specs=row,
        compiler_params=_cparams(1),
        name="final_norm",
    )(h2, norm_w)


def _moe(h2, norm_w, router, w_gate, w_up, w_down, final_w, apply_final):
    n_tok, d = h2.shape
    router_pad = jnp.zeros((d, LANE), F32).at[:, :N_EXPERTS].set(router)
    hn, idx, gate = _route_call(h2, norm_w, jnp.stack(_split2(router_pad)))
    e_flat = idx[:TOP_K].reshape(-1)
    onehot = (e_flat[:, None] == jnp.arange(N_EXPERTS, dtype=jnp.int32)[None, :]).astype(jnp.int32)
    csum = jnp.cumsum(onehot, axis=0)
    rank = jnp.sum(csum * onehot, axis=1) - 1
    counts = csum[-1]
    padded = (counts + MOE_TILE - 1) // MOE_TILE * MOE_TILE
    pad_end = jnp.cumsum(padded)
    pad_start = pad_end - padded
    dest = pad_start[e_flat] + rank
    n_assign = n_tok * TOP_K
    n_blk = -(-n_assign // MOE_TILE) + N_EXPERTS
    blk_start = jnp.arange(n_blk, dtype=jnp.int32) * MOE_TILE
    blk_e = jnp.minimum(jnp.searchsorted(pad_end, blk_start, side="right"), N_EXPERTS - 1).astype(jnp.int32)
    blk_valid = (blk_start < pad_end[-1]).astype(jnp.int32)
    order = jnp.argsort(e_flat, stable=True).astype(jnp.int32)
    tok_sorted = jnp.concatenate([order % n_tok, jnp.zeros((n_assign,), jnp.int32)])
    grp_start = jnp.cumsum(counts) - counts
    buf = jnp.zeros((n_blk * MOE_TILE + n_assign,), jnp.int32)
    for e in range(N_EXPERTS):
        run = lax.dynamic_slice(tok_sorted, (grp_start[e],), (n_assign,))
        buf = lax.dynamic_update_slice(buf, run, (pad_start[e],))
    src_tok = buf[:n_blk * MOE_TILE]
    xs = hn.at[src_tok].get(mode="promise_in_bounds")
    f_tiles = 2 if w_gate.shape[2] % (2 * LANE) == 0 else 1
    ys = _expert_call(blk_e, blk_valid, xs, w_gate, w_up, w_down, f_tiles)
    y01 = ys.at[dest].get(mode="promise_in_bounds")
    g0 = gate[0].reshape(n_tok, 1)
    g1 = gate[1].reshape(n_tok, 1)
    return _combine_call(h2, y01, g0, g1, final_w, apply_final)


def _pad_cols(w, n):
    return jnp.concatenate([w, jnp.zeros((w.shape[0], n - w.shape[1]), w.dtype)], axis=1)


def _token_mix(h, norm_w, w_in, lb, hgrn_norm, s5, s5_d, s5_glu_w, s5_glu_b, dn_conv_w, dn_a_log, dn_dt_bias,
               dn_norm, gla_w_lr, gla_b_lr, gla_norm, w_branch, w_out, rs, tril):
    bsz, seq, d = h.shape
    m = MIX_W
    o = 0
    w_a = w_in[:, o:o + 4 * m]; o += 4 * m
    w_s5 = w_in[:, o:o + m]; o += m
    w_qkv = w_in[:, o:o + 3 * m]; o += 3 * m
    w_bd = w_in[:, o:o + 2 * DN_HEADS]; o += 2 * DN_HEADS
    w_dg = w_in[:, o:o + m]; o += m
    nqk = GLA_HEADS * GLA_DK
    w_gqk = w_in[:, o:o + 2 * nqk]; o += 2 * nqk
    w_gv = w_in[:, o:o + m]; o += m
    w_glr = w_in[:, o:o + GLA_RANK]; o += GLA_RANK
    w_gg = w_in[:, o:o + m]; o += m
    w_m = w_in[:, o:o + N_BRANCH * d]

    w_a = w_a.astype(BF16)
    w_c = jnp.concatenate([w_qkv, w_dg, _pad_cols(w_bd, LANE)], axis=1).astype(BF16)
    w_d = jnp.concatenate([w_gqk, w_gv, w_gg, _pad_cols(w_glr, LANE)], axis=1).astype(BF16)
    row = lambda a: a.reshape(1, -1).astype(F32)
    nw = row(norm_w)

    dummy = jnp.zeros((1, LANE), F32)
    y_a = _gla_call("hgrn", h, nw, w_a, row(lb), dummy, row(hgrn_norm), rs)

    wlr_pad = jnp.zeros((LANE, nqk), F32).at[:GLA_RANK].set(gla_w_lr)
    y_d = _gla_call("gla", h, nw, w_d, wlr_pad, row(gla_b_lr), row(gla_norm), rs)

    lane_row = lambda a: jnp.zeros((1, LANE), F32).at[0, DN_HEADS:2 * DN_HEADS].set(a)
    y_c = _delta_call(h, nw, w_c, dn_conv_w.astype(F32), lane_row(dn_a_log), lane_row(dn_dt_bias), row(dn_norm), tril)

    bre, bim, ar, ai, cre, cim = s5
    y_b = _s5_call(h, nw, w_s5.astype(BF16), bre, bim, ar, ai, cre, cim, row(s5_d), s5_glu_w.astype(BF16),
                   row(s5_glu_b))

    return _merge_call(h, nw, w_m.astype(BF16), y_a, y_b, y_c, y_d, w_branch.astype(BF16), w_out.astype(BF16))


def kernel(x, attn_norm, ffn_norm, final_norm, w_in, hgrn_lb, hgrn_norm, s5_a_re, s5_a_im, s5_log_dt, s5_b_re, s5_b_im, s5_c_re, s5_c_im, s5_d, s5_glu_w, s5_glu_b, dn_conv_w, dn_a_log, dn_dt_bias, dn_norm, gla_w_lr, gla_b_lr, gla_norm, w_branch, w_out, ffn_w_gate, ffn_w_up, ffn_w_down, moe_router, moe_w_gate, moe_w_up, moe_w_down):
    bsz, seq, d = x.shape
    depth = w_in.shape[0]
    assert bsz == SUBLANE, "the S5 kernel keeps one batch row per sublane"
    assert seq % CHUNK == 0 and d == D_MODEL
    lb_soft = jax.nn.softmax(hgrn_lb.astype(F32), axis=0)
    lb_all = jnp.cumsum(lb_soft, axis=0) - lb_soft[0]
    rs = jnp.asarray(_range_sum_matrix(), BF16)
    tril = jnp.asarray(np.tile(np.tril(np.ones((CHUNK, CHUNK), np.float32)), (1, 3)), BF16)
    row = lambda a: a.reshape(1, -1).astype(F32)

    h = x
    for layer in range(depth):
        s5 = _s5_params(s5_a_re[layer], s5_a_im[layer], s5_log_dt[layer], s5_b_re[layer], s5_b_im[layer],
                        s5_c_re[layer], s5_c_im[layer], bsz)
        h = _token_mix(h, attn_norm[layer], w_in[layer], lb_all[layer], hgrn_norm[layer], s5, s5_d[layer],
                       s5_glu_w[layer], s5_glu_b[layer], dn_conv_w[layer], dn_a_log[layer], dn_dt_bias[layer],
                       dn_norm[layer], gla_w_lr[layer], gla_b_lr[layer], gla_norm[layer], w_branch[layer],
                       w_out[layer], rs, tril)
        h2 = h.reshape(bsz * seq, d)
        j = layer // 2
        last = layer == depth - 1
        if layer % 2 == 0:
            f = ffn_w_gate.shape[2]
            f_tiles = f // (2 * LANE) if f % (2 * LANE) == 0 else 1
            h2 = _ffn_call(h2, row(ffn_norm[layer]), ffn_w_gate[j].astype(BF16), ffn_w_up[j].astype(BF16),
                           ffn_w_down[j].astype(BF16), f_tiles)
            if last:
                h2 = _final_norm_call(h2, row(final_norm))
        else:
            h2 = _moe(h2, row(ffn_norm[layer]), moe_router[j], moe_w_gate[j].astype(BF16),
                      moe_w_up[j].astype(BF16), moe_w_down[j].astype(BF16), row(final_norm), last)
        h = h2.reshape(bsz, seq, d)
    return h
```
